```python
import jax
import jax.numpy as jnp
from jax import lax
import numpy as np

D_MODEL = 2048
BATCH = 4
SEQ = 4096
DEPTH = 1

ATTN_HEADS = 8
HEAD_DIM = 128
ATTN_WIDTH = ATTN_HEADS * HEAD_DIM
CONV_WIDTH = D_MODEL - ATTN_WIDTH
CONV_GROUPS = 8
CONV_KERNEL = 31
Q_BLOCK = 128
IN_COLS = 3 * ATTN_WIDTH + ATTN_HEADS + 2 * CONV_WIDTH
N_EXPERTS = 32
TOP_K = 4
D_FF = D_MODEL
SWIGLU_LIMIT = 7.0
SWIGLU_ALPHA = 1.702
EXPERT_BLOCK = 128
EPS = 1e-6

kernel_name = "hybrid_fox_conformer_moe_block"


def rms_norm(x, g):
    xf = x.astype(jnp.float32)
    y = xf * lax.rsqrt(jnp.mean(xf * xf, axis=-1, keepdims=True) + EPS)
    return (y * g.astype(jnp.float32)).astype(x.dtype)


def layer_norm(x, g, b):
    xf = x.astype(jnp.float32)
    mu = jnp.mean(xf, axis=-1, keepdims=True)
    var = jnp.mean(jnp.square(xf - mu), axis=-1, keepdims=True)
    y = (xf - mu) * lax.rsqrt(var + EPS)
    return (y * g.astype(jnp.float32) + b.astype(jnp.float32)).astype(x.dtype)


def fox_attention(q, k, v, cum):
    b, h, s, dh = q.shape
    nb = s // Q_BLOCK
    scale = dh ** -0.5
    q_blocks = q.reshape(b, h, nb, Q_BLOCK, dh).transpose(2, 0, 1, 3, 4)
    c_blocks = cum.reshape(b, h, nb, Q_BLOCK).transpose(2, 0, 1, 3)
    key_pos = jnp.arange(s)

    def one_block(args):
        i, q_i, c_i = args
        logits = jnp.einsum("bhqd,bhkd->bhqk", q_i, k).astype(jnp.float32) * scale
        logits = logits + (c_i[..., :, None] - cum[..., None, :])
        q_pos = i * Q_BLOCK + jnp.arange(Q_BLOCK)
        logits = jnp.where(key_pos[None, :] <= q_pos[:, None], logits, -jnp.inf)
        p = jax.nn.softmax(logits, axis=-1).astype(v.dtype)
        return jnp.einsum("bhqk,bhkd->bhqd", p, v)

    out = lax.map(one_block, (jnp.arange(nb), q_blocks, c_blocks))
    return out.transpose(1, 0, 3, 2, 4).reshape(b, s, h * dh)


def hybrid_mixer(h, w_in, b_f, q_norm_g, k_norm_g, conv_w, conv_b, conv_ln_g, conv_ln_b, w_out):
    b, s, _ = h.shape
    proj = h @ w_in
    q, k, v, f_logit, conv_in = jnp.split(
        proj, [ATTN_WIDTH, 2 * ATTN_WIDTH, 3 * ATTN_WIDTH, 3 * ATTN_WIDTH + ATTN_HEADS], axis=-1)

    q = rms_norm(q.reshape(b, s, ATTN_HEADS, HEAD_DIM), q_norm_g).transpose(0, 2, 1, 3)
    k = rms_norm(k.reshape(b, s, ATTN_HEADS, HEAD_DIM), k_norm_g).transpose(0, 2, 1, 3)
    v = v.reshape(b, s, ATTN_HEADS, HEAD_DIM).transpose(0, 2, 1, 3)
    log_f = jax.nn.log_sigmoid((f_logit + b_f).astype(jnp.float32))
    cum = jnp.cumsum(log_f, axis=1).transpose(0, 2, 1)
    attn_out = fox_attention(q, k, v, cum)

    a, g = jnp.split(conv_in, 2, axis=-1)
    u = a * jax.nn.sigmoid(g)
    u = lax.conv_general_dilated(
        u, conv_w[:, None, :], window_strides=(1,), padding=[(CONV_KERNEL - 1, 0)],
        dimension_numbers=("NWC", "WIO", "NWC"), feature_group_count=CONV_WIDTH) + conv_b
    conv_out = jax.nn.silu(layer_norm(u, conv_ln_g, conv_ln_b))

    return jnp.concatenate([attn_out, conv_out], axis=-1) @ w_out


def moe_ffn(h, w_router, b_router, w_gate, b_gate, w_up, b_up, w_down, b_down):
    b, s, d = h.shape
    xf = h.reshape(b * s, d)
    n_tok = b * s
    n_assign = n_tok * TOP_K
    n_blocks = n_assign // EXPERT_BLOCK + N_EXPERTS
    logits = (xf @ w_router + b_router).astype(jnp.float32)
    top_val, top_idx = lax.top_k(logits, TOP_K)
    top_w = jax.nn.softmax(top_val, axis=-1).astype(h.dtype)

    flat_e = top_idx.reshape(n_assign).astype(jnp.int32)
    flat_tok = jnp.arange(n_assign, dtype=jnp.int32) // TOP_K
    flat_w = top_w.reshape(n_assign)
    order = jnp.argsort(flat_e, stable=True)
    sorted_e = flat_e[order]
    sizes = jnp.bincount(flat_e, length=N_EXPERTS).astype(jnp.int32)
    start = jnp.cumsum(sizes) - sizes
    padded = (sizes + EXPERT_BLOCK - 1) // EXPERT_BLOCK * EXPERT_BLOCK
    padded_end = jnp.cumsum(padded)
    padded_start = padded_end - padded
    dest = padded_start[sorted_e] + jnp.arange(n_assign, dtype=jnp.int32) - start[sorted_e]
    n_slots = n_blocks * EXPERT_BLOCK
    slot_tok = jnp.zeros((n_slots,), jnp.int32).at[dest].set(flat_tok[order])
    slot_w = jnp.zeros((n_slots,), h.dtype).at[dest].set(flat_w[order])
    block_e = jnp.minimum(
        jnp.searchsorted(padded_end, jnp.arange(n_blocks, dtype=jnp.int32) * EXPERT_BLOCK, side="right"),
        N_EXPERTS - 1).astype(jnp.int32)

    def expert_block(y, args):
        tok, wt, e = args
        xb = xf[tok]
        g = jnp.minimum(xb @ w_gate[e] + b_gate[e], SWIGLU_LIMIT)
        u = jnp.clip(xb @ w_up[e] + b_up[e], -SWIGLU_LIMIT, SWIGLU_LIMIT)
        hid = (u + 1) * (g * jax.nn.sigmoid(SWIGLU_ALPHA * g))
        out = hid @ w_down[e] + b_down[e]
        return y.at[tok].add(out * wt[:, None]), None

    y, _ = lax.scan(
        expert_block, jnp.zeros_like(xf),
        (slot_tok.reshape(n_blocks, EXPERT_BLOCK), slot_w.reshape(n_blocks, EXPERT_BLOCK), block_e))
    return y.reshape(b, s, d)


def setup_inputs(seed: int = 0) -> dict:
    key = jax.random.key(seed)
    ks = jax.random.split(key, 24)
    f32 = jnp.float32
    L, D, E, F, H, C = DEPTH, D_MODEL, N_EXPERTS, D_FF, ATTN_HEADS, CONV_WIDTH

    def nrm(k, shape, fan_in, mult=1.0):
        return jax.random.normal(k, shape, f32) * (mult * fan_in ** -0.5)

    def small(k, shape, s=0.02):
        return jax.random.normal(k, shape, f32) * s

    return {
        "x": jax.random.normal(ks[0], (BATCH, SEQ, D), f32),
        "c": jax.random.normal(ks[1], (BATCH, D), f32),
        "ada_w": nrm(ks[2], (L, D, 6 * D), D, 0.5),
        "ada_b": small(ks[3], (L, 6 * D)),
        "norm_mix_g": 1.0 + small(ks[4], (L, D), 0.1),
        "norm_ffn_g": 1.0 + small(ks[5], (L, D), 0.1),
        "w_in": nrm(ks[6], (L, D, IN_COLS), D),
        "b_f": jax.random.uniform(ks[7], (L, H), f32, 1.0, 5.0),
        "q_norm_g": 1.0 + small(ks[8], (L, HEAD_DIM), 0.1),
        "k_norm_g": 1.0 + small(ks[9], (L, HEAD_DIM), 0.1),
        "conv_w": nrm(ks[10], (L, CONV_KERNEL, C), CONV_KERNEL),
        "conv_b": small(ks[11], (L, C)),
        "conv_ln_g": 1.0 + small(ks[12], (L, C), 0.1),
        "conv_ln_b": small(ks[13], (L, C)),
        "w_out": nrm(ks[14], (L, D, D), D),
        "w_router": nrm(ks[15], (L, D, E), D),
        "b_router": small(ks[16], (L, E), 0.01),
        "w_gate": nrm(ks[17], (L, E, D, F), D),
        "b_gate": small(ks[18], (L, E, F)),
        "w_up": nrm(ks[19], (L, E, D, F), D),
        "b_up": small(ks[20], (L, E, F)),
        "w_down": nrm(ks[21], (L, E, F, D), F),
        "b_down": small(ks[22], (L, E, D)),
    }


def reference(x, c, ada_w, ada_b, norm_mix_g, norm_ffn_g, w_in, b_f, q_norm_g, k_norm_g,
              conv_w, conv_b, conv_ln_g, conv_ln_b, w_out, w_router, b_router,
              w_gate, b_gate, w_up, b_up, w_down, b_down):
    c_act = jax.nn.silu(c)
    for l in range(DEPTH):
        mod = (c_act @ ada_w[l] + ada_b[l])[:, None, :]
        shift1, scale1, gate1, shift2, scale2, gate2 = jnp.split(mod, 6, axis=-1)
        h = rms_norm(x, norm_mix_g[l]) * (1 + scale1) + shift1
        x = x + gate1 * hybrid_mixer(h, w_in[l], b_f[l], q_norm_g[l], k_norm_g[l], conv_w[l], conv_b[l],
                                     conv_ln_g[l], conv_ln_b[l], w_out[l])
        h = rms_norm(x, norm_ffn_g[l]) * (1 + scale2) + shift2
        x = x + gate2 * moe_ffn(h, w_router[l], b_router[l], w_gate[l], b_gate[l], w_up[l], b_up[l],
                                w_down[l], b_down[l])
    return x
```

```python
import functools

import jax
import jax.numpy as jnp
from jax import lax
from jax.experimental import pallas as pl
from jax.experimental.pallas import tpu as pltpu

F32 = jnp.float32
BF16 = jnp.bfloat16
I32 = jnp.int32

D_MODEL = 2048
ATTN_HEADS = 8
HEAD_DIM = 128
ATTN_WIDTH = ATTN_HEADS * HEAD_DIM
CONV_WIDTH = D_MODEL - ATTN_WIDTH
CONV_KERNEL = 31
N_EXPERTS = 32
TOP_K = 4
SWIGLU_LIMIT = 7.0
SWIGLU_ALPHA = 1.702
EPS = 1e-6

LANES = 128
SUBLANES = 8
ROW_SLAB = D_MODEL // LANES
VMEM_LIMIT = 56 * 1024 * 1024

TM_PROJ = 512
TQ = 512
CUM_CHUNK = 256
TS_CONV = 256
CONV_HALO = 32
CONV_CHUNK = 32
TM_ROUTE = 512
TM_MOE = 256
TM_COMB = 256
NEG_BIG = -1e30


def _params(n_axes):
    return pltpu.CompilerParams(
        dimension_semantics=("arbitrary",) * n_axes, vmem_limit_bytes=VMEM_LIMIT)


def _resident(shape):
    nd = len(shape)
    return pl.BlockSpec(shape, lambda *_: (0,) * nd, pipeline_mode=pl.Buffered(1))


def _ada_kernel(c_ref, w_ref, b_ref, o_ref):
    c = c_ref[...]
    c_act = (c * jax.nn.sigmoid(c)).astype(BF16)
    o_ref[...] = jnp.dot(c_act, w_ref[...].astype(BF16), preferred_element_type=F32) + b_ref[...]


def _ada_mod(c_pad, ada_w, ada_b):
    rows, d = c_pad.shape
    n = ada_w.shape[1]
    tn = 1024
    return pl.pallas_call(
        _ada_kernel,
        grid=(n // tn,),
        in_specs=[pl.BlockSpec((rows, d), lambda j: (0, 0)),
                  pl.BlockSpec((d, tn), lambda j: (0, j)),
                  pl.BlockSpec((1, tn), lambda j: (0, j))],
        out_specs=pl.BlockSpec((rows, tn), lambda j: (0, j)),
        out_shape=jax.ShapeDtypeStruct((rows, n), F32),
        compiler_params=_params(1),
        name="ada_mod",
    )(c_pad, ada_w, ada_b.reshape(1, n))


def _log_sigmoid(x):
    return jnp.minimum(x, 0.0) - jnp.log1p(jnp.exp(-jnp.abs(x)))


def _head_rms(y, g):
    outs = []
    for h in range(ATTN_HEADS):
        yh = y[:, h * HEAD_DIM:(h + 1) * HEAD_DIM]
        r = lax.rsqrt(jnp.mean(yh * yh, axis=-1, keepdims=True) + EPS)
        outs.append(yh * r * g)
    return jnp.concatenate(outs, axis=-1)


def _inproj_kernel(x_ref, a_ref, s_ref, w_ref, wf_ref, bf_ref, qg_ref, kg_ref,
                   q_ref, k_ref, v_ref, u_ref, f_ref):
    x = x_ref[0]
    h = x * lax.rsqrt(jnp.mean(x * x, axis=-1, keepdims=True) + EPS) * a_ref[0] + s_ref[0]
    hb = h.astype(BF16)
    aw = ATTN_WIDTH
    q = jnp.dot(hb, w_ref[:, 0:aw], preferred_element_type=F32)
    q_ref[0] = (_head_rms(q, qg_ref[...]) * (HEAD_DIM ** -0.5)).astype(BF16)
    k = jnp.dot(hb, w_ref[:, aw:2 * aw], preferred_element_type=F32)
    k_ref[0] = _head_rms(k, kg_ref[...]).astype(BF16)
    v_ref[0] = jnp.dot(hb, w_ref[:, 2 * aw:3 * aw], preferred_element_type=F32).astype(BF16)
    a = jnp.dot(hb, w_ref[:, 3 * aw:3 * aw + CONV_WIDTH], preferred_element_type=F32)
    g = jnp.dot(hb, w_ref[:, 3 * aw + CONV_WIDTH:], preferred_element_type=F32)
    u_ref[0] = (a * jax.nn.sigmoid(g)).astype(BF16)
    fl = jnp.dot(hb, wf_ref[...], preferred_element_type=F32) + bf_ref[...]
    f_ref[0] = _log_sigmoid(fl)


def _in_proj(x, a1, s1, w_cat, w_f, b_f, q_g, k_g):
    b, s, d = x.shape
    tm = TM_PROJ
    ncat = w_cat.shape[1]
    tok = lambda w: pl.BlockSpec((1, tm, w), lambda bi, i: (bi, i, 0))
    per_batch = pl.BlockSpec((1, 1, d), lambda bi, i: (bi, 0, 0))
    return pl.pallas_call(
        _inproj_kernel,
        grid=(b, s // tm),
        in_specs=[tok(d), per_batch, per_batch,
                  _resident((d, ncat)), _resident((d, LANES)), _resident((1, LANES)),
                  _resident((1, HEAD_DIM)), _resident((1, HEAD_DIM))],
        out_specs=[tok(ATTN_WIDTH), tok(ATTN_WIDTH), tok(ATTN_WIDTH), tok(CONV_WIDTH), tok(LANES)],
        out_shape=[jax.ShapeDtypeStruct((b, s, ATTN_WIDTH), BF16)] * 3
        + [jax.ShapeDtypeStruct((b, s, CONV_WIDTH), BF16),
           jax.ShapeDtypeStruct((b, s, LANES), F32)],
        compiler_params=_params(2),
        name="in_proj",
    )(x, a1, s1, w_cat, w_f, b_f, q_g, k_g)


def _cumsum_kernel(f_ref, c_ref, ct_ref):
    ch = CUM_CHUNK
    s = f_ref.shape[1]
    row = lax.broadcasted_iota(I32, (ch, ch), 0)
    col = lax.broadcasted_iota(I32, (ch, ch), 1)
    tri = (col <= row).astype(BF16)
    carry = jnp.zeros((1, LANES), F32)
    for i in range(s // ch):
        blk = f_ref[0, i * ch:(i + 1) * ch, :]
        p0 = blk.astype(BF16)
        r0 = blk - p0.astype(F32)
        p1 = r0.astype(BF16)
        p2 = (r0 - p1.astype(F32)).astype(BF16)
        cs = (jnp.dot(tri, p0, preferred_element_type=F32)
              + jnp.dot(tri, p1, preferred_element_type=F32)
              + jnp.dot(tri, p2, preferred_element_type=F32)) + carry
        c_ref[0, i * ch:(i + 1) * ch, :] = cs
        ct_ref[0, :, i * ch:(i + 1) * ch] = cs.T[0:SUBLANES, :]
        carry = cs[ch - 1:ch, :]


def _forget_cumsum(logf):
    b, s, _ = logf.shape
    return pl.pallas_call(
        _cumsum_kernel,
        grid=(b,),
        in_specs=[pl.BlockSpec((1, s, LANES), lambda bi: (bi, 0, 0))],
        out_specs=[pl.BlockSpec((1, s, LANES), lambda bi: (bi, 0, 0)),
                   pl.BlockSpec((1, SUBLANES, s), lambda bi: (bi, 0, 0))],
        out_shape=[jax.ShapeDtypeStruct((b, s, LANES), F32),
                   jax.ShapeDtypeStruct((b, SUBLANES, s), F32)],
        compiler_params=_params(1),
        name="forget_cumsum",
    )(logf)


def _attn_kernel(q_ref, k_ref, v_ref, cq_ref, ck_ref, o_ref, m_sc, l_sc, acc_sc):
    h = pl.program_id(1)
    qi = pl.program_id(2)
    tq = TQ
    q = q_ref[0]
    lane = lax.broadcasted_iota(I32, (tq, LANES), 1)
    cq = jnp.sum(jnp.where(lane == h, cq_ref[0], 0.0), axis=-1, keepdims=True)
    m_sc[...] = jnp.full(m_sc.shape, -jnp.inf, F32)
    l_sc[...] = jnp.zeros(l_sc.shape, F32)
    acc_sc[...] = jnp.zeros(acc_sc.shape, F32)

    def step(j, masked):
        start = pl.multiple_of(j * tq, tq)
        kj = k_ref[0, pl.ds(start, tq), :]
        vj = v_ref[0, pl.ds(start, tq), :]
        ck = ck_ref[0, :, pl.ds(start, tq)]
        s = lax.dot_general(q, kj, (((1,), (1,)), ((), ())), preferred_element_type=F32)
        s = s + (cq - ck)
        if masked:
            row = lax.broadcasted_iota(I32, (tq, tq), 0)
            col = lax.broadcasted_iota(I32, (tq, tq), 1)
            s = jnp.where(col <= row, s, -jnp.inf)
        m_prev = m_sc[...]
        m_new = jnp.maximum(m_prev, jnp.max(s, axis=-1, keepdims=True))
        alpha = jnp.exp(m_prev - m_new)
        p = jnp.exp(s - m_new)
        l_sc[...] = alpha * l_sc[...] + jnp.sum(p, axis=-1, keepdims=True)
        acc_sc[...] = alpha * acc_sc[...] + jnp.dot(p.astype(BF16), vj, preferred_element_type=F32)
        m_sc[...] = m_new

    def body(j, carry):
        step(j, False)
        return carry

    lax.fori_loop(0, qi, body, 0)
    step(qi, True)
    o_ref[0] = (acc_sc[...] / l_sc[...]).astype(BF16)


def _fox_attention(q, k, v, cum, cum_t):
    b, s, _ = q.shape
    tq = TQ
    return pl.pallas_call(
        _attn_kernel,
        grid=(b, ATTN_HEADS, s // tq),
        in_specs=[pl.BlockSpec((1, tq, HEAD_DIM), lambda bi, h, i: (bi, i, h)),
                  pl.BlockSpec((1, s, HEAD_DIM), lambda bi, h, i: (bi, 0, h)),
                  pl.BlockSpec((1, s, HEAD_DIM), lambda bi, h, i: (bi, 0, h)),
                  pl.BlockSpec((1, tq, LANES), lambda bi, h, i: (bi, i, 0)),
                  pl.BlockSpec((1, 1, s), lambda bi, h, i: (bi * ATTN_HEADS + h, 0, 0))],
        out_specs=pl.BlockSpec((1, tq, HEAD_DIM), lambda bi, h, i: (bi, i, h)),
        out_shape=jax.ShapeDtypeStruct((b, s, ATTN_WIDTH), BF16),
        scratch_shapes=[pltpu.VMEM((tq, 1), F32), pltpu.VMEM((tq, 1), F32),
                        pltpu.VMEM((tq, HEAD_DIM), F32)],
        compiler_params=_params(3),
        name="fox_attention",
    )(q, k, v, cum, cum_t)


def _conv_kernel(u_ref, w_ref, cb_ref, lg_ref, lb_ref, o_ref, ubuf, wb):
    i = pl.program_id(1)
    ts, halo, ck = TS_CONV, CONV_HALO, CONV_CHUNK

    @pl.when(i == 0)
    def _():
        ubuf[0:halo, :] = jnp.zeros((halo, CONV_WIDTH), F32)
        for j in range(CONV_KERNEL):
            wb[j * SUBLANES:(j + 1) * SUBLANES, :] = jnp.broadcast_to(
                w_ref[j:j + 1, :], (SUBLANES, CONV_WIDTH))

    ubuf[halo:halo + ts, :] = u_ref[0].astype(F32)
    base = halo - (CONV_KERNEL - 1)
    for c in range(ts // ck):
        acc = jnp.zeros((ck, CONV_WIDTH), F32)
        for j in range(CONV_KERNEL):
            off = c * ck + base + j
            wj = wb[j * SUBLANES:(j + 1) * SUBLANES, :]
            acc = acc + ubuf[off:off + ck, :] * jnp.concatenate([wj] * (ck // SUBLANES), axis=0)
        y = acc + cb_ref[...]
        mu = jnp.mean(y, axis=-1, keepdims=True)
        yc = y - mu
        var = jnp.mean(yc * yc, axis=-1, keepdims=True)
        z = yc * lax.rsqrt(var + EPS) * lg_ref[...] + lb_ref[...]
        o_ref[0, c * ck:(c + 1) * ck, :] = (z * jax.nn.sigmoid(z)).astype(BF16)
    ubuf[0:halo, :] = ubuf[ts:ts + halo, :]


def _conv_module(u, conv_w, conv_b, ln_g, ln_b):
    b, s, cw = u.shape
    ts = TS_CONV
    return pl.pallas_call(
        _conv_kernel,
        grid=(b, s // ts),
        in_specs=[pl.BlockSpec((1, ts, cw), lambda bi, i: (bi, i, 0)),
                  _resident((CONV_KERNEL, cw)), _resident((1, cw)),
                  _resident((1, cw)), _resident((1, cw))],
        out_specs=pl.BlockSpec((1, ts, cw), lambda bi, i: (bi, i, 0)),
        out_shape=jax.ShapeDtypeStruct((b, s, cw), BF16),
        scratch_shapes=[pltpu.VMEM((CONV_HALO + ts, cw), F32),
                        pltpu.VMEM((CONV_KERNEL * SUBLANES, cw), F32)],
        compiler_params=_params(2),
        name="conv_module",
    )(u, conv_w, conv_b, ln_g, ln_b)


def _store_row_slabs(ref, val):
    rows = val.shape[0]
    for s in range(ROW_SLAB):
        ref[pl.ds(s, rows, stride=ROW_SLAB), :] = val[:, s * LANES:(s + 1) * LANES]


def _load_row_slabs(ref, start, rows):
    parts = [ref[pl.ds(start * ROW_SLAB + s, rows, stride=ROW_SLAB), :] for s in range(ROW_SLAB)]
    return jnp.concatenate(parts, axis=-1)


def _outproj_kernel(at_ref, cv_ref, x_ref, wo_ref, g1_ref, a2_ref, s2_ref, wrh_ref, wrl_ref, br_ref,
                    x1_ref, z_ref, lg_ref):
    mix = (jnp.dot(at_ref[0], wo_ref[0:ATTN_WIDTH, :], preferred_element_type=F32)
           + jnp.dot(cv_ref[0], wo_ref[ATTN_WIDTH:, :], preferred_element_type=F32))
    x1 = x_ref[0] + g1_ref[0] * mix
    x1_ref[0] = x1
    h2 = x1 * lax.rsqrt(jnp.mean(x1 * x1, axis=-1, keepdims=True) + EPS) * a2_ref[0] + s2_ref[0]
    _store_row_slabs(z_ref, h2)
    hi = h2.astype(BF16)
    lo = (h2 - hi.astype(F32)).astype(BF16)
    lg_ref[0] = (jnp.dot(hi, wrh_ref[...], preferred_element_type=F32)
                 + jnp.dot(lo, wrh_ref[...], preferred_element_type=F32)
                 + jnp.dot(hi, wrl_ref[...], preferred_element_type=F32)) + br_ref[...]


def _out_proj(attn, conv, x, w_out, g1, a2, s2, wr_hi, wr_lo, b_r):
    b, s, d = x.shape
    tm = TM_PROJ
    nt = s // tm
    tok = lambda w: pl.BlockSpec((1, tm, w), lambda bi, i: (bi, i, 0))
    per_batch = pl.BlockSpec((1, 1, d), lambda bi, i: (bi, 0, 0))
    return pl.pallas_call(
        _outproj_kernel,
        grid=(b, nt),
        in_specs=[tok(ATTN_WIDTH), tok(CONV_WIDTH), tok(d), _resident((d, d)),
                  per_batch, per_batch, per_batch,
                  _resident((d, LANES)), _resident((d, LANES)), _resident((1, LANES))],
        out_specs=[tok(d),
                   pl.BlockSpec((tm * ROW_SLAB, LANES), lambda bi, i: (bi * nt + i, 0)),
                   tok(LANES)],
        out_shape=[jax.ShapeDtypeStruct((b, s, d), F32),
                   jax.ShapeDtypeStruct((b * s * ROW_SLAB, LANES), F32),
                   jax.ShapeDtypeStruct((b, s, LANES), F32)],
        compiler_params=_params(2),
        name="out_proj",
    )(attn, conv, x, w_out, g1, a2, s2, wr_hi, wr_lo, b_r)


def _route_kernel(lg_ref, ri_ref, rw_ref, cnt_ref, carry):
    i = pl.program_id(0)
    tm = TM_ROUTE

    @pl.when(i == 0)
    def _():
        carry[...] = jnp.zeros(carry.shape, F32)

    l = lg_ref[...]
    lane = lax.broadcasted_iota(I32, (tm, LANES), 1).astype(F32)
    vals, idxs = [], []
    for _ in range(TOP_K):
        m = jnp.max(l, axis=-1, keepdims=True)
        ix = jnp.min(jnp.where(l == m, lane, float(LANES)), axis=-1, keepdims=True)
        vals.append(m)
        idxs.append(ix)
        l = jnp.where(lane == ix, -jnp.inf, l)
    es = [jnp.exp(v - vals[0]) for v in vals]
    den = es[0] + es[1] + es[2] + es[3]
    onehot = jnp.zeros((tm, LANES), F32)
    for ix in idxs:
        onehot = onehot + jnp.where(lane == ix, 1.0, 0.0)
    row = lax.broadcasted_iota(I32, (tm, tm), 0)
    col = lax.broadcasted_iota(I32, (tm, tm), 1)
    strict = (col < row).astype(BF16)
    before = jnp.dot(strict, onehot.astype(BF16), preferred_element_type=F32) + carry[0:1, :]
    ri = jnp.zeros((tm, LANES), F32)
    rw = jnp.zeros((tm, LANES), F32)
    for k in range(TOP_K):
        rank = jnp.sum(jnp.where(lane == idxs[k], before, 0.0), axis=-1, keepdims=True)
        ri = jnp.where(lane == k, idxs[k], ri)
        ri = jnp.where(lane == TOP_K + k, rank, ri)
        rw = jnp.where(lane == k, es[k] / den, rw)
    ri_ref[...] = ri.astype(I32)
    rw_ref[...] = rw
    carry[0:1, :] = carry[0:1, :] + jnp.sum(onehot, axis=0, keepdims=True)
    cnt_ref[...] = carry[...]


def _route(logits):
    t = logits.shape[0]
    tm = TM_ROUTE
    return pl.pallas_call(
        _route_kernel,
        grid=(t // tm,),
        in_specs=[pl.BlockSpec((tm, LANES), lambda i: (i, 0))],
        out_specs=[pl.BlockSpec((tm, LANES), lambda i: (i, 0)),
                   pl.BlockSpec((tm, LANES), lambda i: (i, 0)),
                   pl.BlockSpec((SUBLANES, LANES), lambda i: (0, 0))],
        out_shape=[jax.ShapeDtypeStruct((t, LANES), I32),
                   jax.ShapeDtypeStruct((t, LANES), F32),
                   jax.ShapeDtypeStruct((SUBLANES, LANES), F32)],
        scratch_shapes=[pltpu.VMEM((SUBLANES, LANES), F32)],
        compiler_params=_params(1),
        name="route",
    )(logits)


def _issue_row_gather(idx_ref, idx_base, n_rows, src_hbm, dst_buf, sem):
    def body(r, carry):
        t = idx_ref[idx_base + r]
        pltpu.make_async_copy(
            src_hbm.at[pl.ds(pl.multiple_of(t * ROW_SLAB, ROW_SLAB), ROW_SLAB), :],
            dst_buf.at[pl.ds(pl.multiple_of(r * ROW_SLAB, ROW_SLAB), ROW_SLAB), :],
            sem).start()
        return carry
    lax.fori_loop(0, n_rows, body, 0)


def _wait_row_gather(n_rows, src_hbm, dst_buf, sem):
    pltpu.make_async_copy(src_hbm.at[pl.ds(0, n_rows * ROW_SLAB), :], dst_buf, sem).wait()


def _moe_up_kernel(be_ref, tok_ref, nu_ref, z_hbm, wg_ref, bg_ref, wu_ref, bu_ref, hid_ref, zbuf, sem):
    b = pl.program_id(0)
    n_used = nu_ref[0]
    tm = TM_MOE

    @pl.when(b == 0)
    def _():
        _issue_row_gather(tok_ref, 0, tm, z_hbm, zbuf.at[0], sem.at[0])

    @pl.when(b + 1 < n_used)
    def _():
        nxt = (b + 1) % 2
        _issue_row_gather(tok_ref, (b + 1) * tm, tm, z_hbm, zbuf.at[nxt], sem.at[nxt])

    @pl.when(b < n_used)
    def _():
        cur = b % 2
        _wait_row_gather(tm, z_hbm, zbuf.at[cur], sem.at[cur])
        x = _load_row_slabs(zbuf.at[cur], 0, tm).astype(BF16)
        g = jnp.dot(x, wg_ref[0], preferred_element_type=F32) + bg_ref[0]
        u = jnp.dot(x, wu_ref[0], preferred_element_type=F32) + bu_ref[0]
        g = jnp.minimum(g, SWIGLU_LIMIT)
        u = jnp.clip(u, -SWIGLU_LIMIT, SWIGLU_LIMIT)
        hid_ref[...] = ((u + 1.0) * (g * jax.nn.sigmoid(SWIGLU_ALPHA * g))).astype(BF16)

    @pl.when(b >= n_used)
    def _():
        hid_ref[...] = jnp.zeros(hid_ref.shape, BF16)


def _moe_up(block_e, slot_tok, n_used, z, w_gate, b_gate, w_up, b_up):
    e, d, f = w_gate.shape
    n_blocks = block_e.shape[0]
    tm = TM_MOE
    wspec = pl.BlockSpec((1, d, f), lambda b, be, tok, nu: (be[b], 0, 0))
    bspec = pl.BlockSpec((1, 1, f), lambda b, be, tok, nu: (be[b], 0, 0))
    return pl.pallas_call(
        _moe_up_kernel,
        grid_spec=pltpu.PrefetchScalarGridSpec(
            num_scalar_prefetch=3,
            grid=(n_blocks,),
            in_specs=[pl.BlockSpec(memory_space=pl.ANY), wspec, bspec, wspec, bspec],
            out_specs=pl.BlockSpec((tm, f), lambda b, be, tok, nu: (b, 0)),
            scratch_shapes=[pltpu.VMEM((2, tm * ROW_SLAB, LANES), F32),
                            pltpu.SemaphoreType.DMA((2,))]),
        out_shape=jax.ShapeDtypeStruct((n_blocks * tm, f), BF16),
        compiler_params=_params(1),
        name="moe_up",
    )(block_e, slot_tok, n_used, z, w_gate, b_gate, w_up, b_up)


def _moe_down_kernel(be_ref, nu_ref, hid_ref, wd_ref, bd_ref, ys_ref):
    b = pl.program_id(0)

    @pl.when(b < nu_ref[0])
    def _():
        out = jnp.dot(hid_ref[...], wd_ref[0], preferred_element_type=F32) + bd_ref[0]
        _store_row_slabs(ys_ref, out)

    @pl.when(b >= nu_ref[0])
    def _():
        ys_ref[...] = jnp.zeros(ys_ref.shape, F32)


def _moe_down(block_e, n_used, hid, w_down, b_down):
    e, f, d = w_down.shape
    n_blocks = block_e.shape[0]
    tm = TM_MOE
    return pl.pallas_call(
        _moe_down_kernel,
        grid_spec=pltpu.PrefetchScalarGridSpec(
            num_scalar_prefetch=2,
            grid=(n_blocks,),
            in_specs=[pl.BlockSpec((tm, f), lambda b, be, nu: (b, 0)),
                      pl.BlockSpec((1, f, d), lambda b, be, nu: (be[b], 0, 0)),
                      pl.BlockSpec((1, 1, d), lambda b, be, nu: (be[b], 0, 0))],
            out_specs=pl.BlockSpec((tm * ROW_SLAB, LANES), lambda b, be, nu: (b, 0))),
        out_shape=jax.ShapeDtypeStruct((n_blocks * tm * ROW_SLAB, LANES), F32),
        compiler_params=_params(1),
        name="moe_down",
    )(block_e, n_used, hid, w_down, b_down)


def _combine_kernel(dest_ref, ys_hbm, x1_ref, rw_ref, g2_ref, o_ref, buf, sem):
    i = pl.program_id(0)
    n = pl.num_programs(0)
    tm = TM_COMB
    rows = TOP_K * tm

    @pl.when(i == 0)
    def _():
        _issue_row_gather(dest_ref, 0, rows, ys_hbm, buf.at[0], sem.at[0])

    @pl.when(i + 1 < n)
    def _():
        nxt = (i + 1) % 2
        _issue_row_gather(dest_ref, (i + 1) * rows, rows, ys_hbm, buf.at[nxt], sem.at[nxt])

    cur = i % 2
    _wait_row_gather(rows, ys_hbm, buf.at[cur], sem.at[cur])
    rw = rw_ref[...]
    y = jnp.zeros((tm, D_MODEL), F32)
    for k in range(TOP_K):
        y = y + rw[:, k:k + 1] * _load_row_slabs(buf.at[cur], k * tm, tm)
    o_ref[...] = x1_ref[...] + g2_ref[0] * y


def _combine(dest_km, ys, x1, rw, g2, seq):
    t, d = x1.shape
    tm = TM_COMB
    per_seq = seq // tm
    return pl.pallas_call(
        _combine_kernel,
        grid_spec=pltpu.PrefetchScalarGridSpec(
            num_scalar_prefetch=1,
            grid=(t // tm,),
            in_specs=[pl.BlockSpec(memory_space=pl.ANY),
                      pl.BlockSpec((tm, d), lambda i, dr: (i, 0)),
                      pl.BlockSpec((tm, LANES), lambda i, dr: (i, 0)),
                      pl.BlockSpec((1, 1, d), lambda i, dr: (i // per_seq, 0, 0))],
            out_specs=pl.BlockSpec((tm, d), lambda i, dr: (i, 0)),
            scratch_shapes=[pltpu.VMEM((2, TOP_K * tm * ROW_SLAB, LANES), F32),
                            pltpu.SemaphoreType.DMA((2,))]),
        out_shape=jax.ShapeDtypeStruct((t, d), F32),
        compiler_params=_params(1),
        name="combine",
    )(dest_km, ys, x1, rw, g2)


def _pad_cols(w, n):
    return jnp.pad(w, ((0, 0), (0, n - w.shape[1])))


def _layer(x, mod, norm_mix_g, norm_ffn_g, w_in, b_f, q_norm_g, k_norm_g, conv_w, conv_b,
           conv_ln_g, conv_ln_b, w_out, w_router, b_router, w_gate, b_gate, w_up, b_up,
           w_down, b_down):
    b, s, d = x.shape
    t = b * s
    shift1, scale1, gate1, shift2, scale2, gate2 = [m[:, None, :] for m in jnp.split(mod, 6, axis=-1)]
    a1 = norm_mix_g[None, None, :] * (1.0 + scale1)
    a2 = norm_ffn_g[None, None, :] * (1.0 + scale2)

    aw = ATTN_WIDTH
    w_cat = jnp.concatenate([w_in[:, :3 * aw], w_in[:, 3 * aw + ATTN_HEADS:]], axis=1).astype(BF16)
    w_f = _pad_cols(w_in[:, 3 * aw:3 * aw + ATTN_HEADS], LANES).astype(BF16)
    b_f_pad = _pad_cols(b_f[None, :], LANES)

    q, k, v, u, logf = _in_proj(x, a1, shift1, w_cat, w_f, b_f_pad,
                                q_norm_g[None, :], k_norm_g[None, :])
    cum, cum_t = _forget_cumsum(logf)
    attn = _fox_attention(q, k, v, cum, cum_t.reshape(b * SUBLANES, 1, s))
    conv = _conv_module(u, conv_w, conv_b[None, :], conv_ln_g[None, :], conv_ln_b[None, :])

    wr = _pad_cols(w_router, LANES)
    wr_hi = wr.astype(BF16)
    wr_lo = (wr - wr_hi.astype(F32)).astype(BF16)
    b_r = jnp.concatenate([b_router, jnp.full((LANES - N_EXPERTS,), NEG_BIG, F32)])[None, :]
    x1, z, logits = _out_proj(attn, conv, x, w_out.astype(BF16), gate1, a2, shift2, wr_hi, wr_lo, b_r)

    ri, rw, cnt = _route(logits.reshape(t, LANES))

    tm = TM_MOE
    n_blocks = t * TOP_K // tm + N_EXPERTS
    idx = ri[:, 0:TOP_K]
    rank = ri[:, TOP_K:2 * TOP_K]
    counts = cnt[0, :N_EXPERTS].astype(I32)
    padded = (counts + tm - 1) // tm * tm
    padded_end = jnp.cumsum(padded)
    padded_start = padded_end - padded
    dest = padded_start[idx] + rank
    slot_tok = jnp.zeros((n_blocks * tm,), I32).at[dest.reshape(-1)].set(
        jnp.repeat(jnp.arange(t, dtype=I32), TOP_K))
    block_e = jnp.minimum(
        jnp.searchsorted(padded_end, jnp.arange(n_blocks, dtype=I32) * tm, side="right"),
        N_EXPERTS - 1).astype(I32)
    n_used = (padded_end[-1:] // tm).astype(I32)
    dest_km = dest.reshape(t // TM_COMB, TM_COMB, TOP_K).transpose(0, 2, 1).reshape(-1)

    hid = _moe_up(block_e, slot_tok, n_used, z, w_gate.astype(BF16), b_gate[:, None, :],
                  w_up.astype(BF16), b_up[:, None, :])
    ys = _moe_down(block_e, n_used, hid, w_down.astype(BF16), b_down[:, None, :])
    out = _combine(dest_km, ys, x1.reshape(t, d), rw, gate2, s)
    return out.reshape(b, s, d)


def kernel(x, c, ada_w, ada_b, norm_mix_g, norm_ffn_g, w_in, b_f, q_norm_g, k_norm_g, conv_w, conv_b,
           conv_ln_g, conv_ln_b, w_out, w_router, b_router, w_gate, b_gate, w_up, b_up, w_down, b_down):
    b = x.shape[0]
    c_pad = jnp.pad(c, ((0, SUBLANES - b), (0, 0)))
    for l in range(ada_w.shape[0]):
        mod = _ada_mod(c_pad, ada_w[l], ada_b[l])[:b]
        x = _layer(x, mod, norm_mix_g[l], norm_ffn_g[l], w_in[l], b_f[l], q_norm_g[l], k_norm_g[l],
                   conv_w[l], conv_b[l], conv_ln_g[l], conv_ln_b[l], w_out[l], w_router[l],
                   b_router[l], w_gate[l], b_gate[l], w_up[l], b_up[l], w_down[l], b_down[l])
    return x
```

```python
import functools

import jax
import jax.numpy as jnp
from jax import lax
from jax.experimental import pallas as pl
from jax.experimental.pallas import tpu as pltpu

F32 = jnp.float32
BF16 = jnp.bfloat16
I32 = jnp.int32

D_MODEL = 2048
ATTN_HEADS = 8
HEAD_DIM = 128
ATTN_WIDTH = ATTN_HEADS * HEAD_DIM
CONV_WIDTH = D_MODEL - ATTN_WIDTH
CONV_KERNEL = 31
N_EXPERTS = 32
TOP_K = 4
SWIGLU_LIMIT = 7.0
SWIGLU_ALPHA = 1.702
EPS = 1e-6

LANES = 128
SUBLANES = 8
ROW_SLAB = D_MODEL // LANES
VMEM_LIMIT = 56 * 1024 * 1024

TM_PROJ = 512
TQ = 512
CUM_CHUNK = 256
TS_CONV = 256
CONV_HALO = 32
CONV_CHUNK = 32
TM_ROUTE = 512
TM_MOE = 256
TM_COMB = 256
GATHER_UNROLL = 32
NEG_BIG = -1e30


def _params(n_axes):
    return pltpu.CompilerParams(
        dimension_semantics=("arbitrary",) * n_axes, vmem_limit_bytes=VMEM_LIMIT)


def _resident(shape):
    nd = len(shape)
    return pl.BlockSpec(shape, lambda *_: (0,) * nd, pipeline_mode=pl.Buffered(1))


def _ada_kernel(c_ref, w_ref, b_ref, o_ref):
    c = c_ref[...]
    c_act = (c * jax.nn.sigmoid(c)).astype(BF16)
    o_ref[...] = jnp.dot(c_act, w_ref[...].astype(BF16), preferred_element_type=F32) + b_ref[...]


def _ada_mod(c_pad, ada_w, ada_b):
    rows, d = c_pad.shape
    n = ada_w.shape[1]
    tn = 1024
    return pl.pallas_call(
        _ada_kernel,
        grid=(n // tn,),
        in_specs=[pl.BlockSpec((rows, d), lambda j: (0, 0)),
                  pl.BlockSpec((d, tn), lambda j: (0, j)),
                  pl.BlockSpec((1, tn), lambda j: (0, j))],
        out_specs=pl.BlockSpec((rows, tn), lambda j: (0, j)),
        out_shape=jax.ShapeDtypeStruct((rows, n), F32),
        compiler_params=_params(1),
        name="ada_mod",
    )(c_pad, ada_w, ada_b.reshape(1, n))


def _log_sigmoid(x):
    return jnp.minimum(x, 0.0) - jnp.log1p(jnp.exp(-jnp.abs(x)))


def _head_rms(y, g):
    outs = []
    for h in range(ATTN_HEADS):
        yh = y[:, h * HEAD_DIM:(h + 1) * HEAD_DIM]
        r = lax.rsqrt(jnp.mean(yh * yh, axis=-1, keepdims=True) + EPS)
        outs.append(yh * r * g)
    return jnp.concatenate(outs, axis=-1)


def _inproj_kernel(x_ref, a_ref, s_ref, w_ref, wf_ref, bf_ref, qg_ref, kg_ref,
                   q_ref, k_ref, v_ref, u_ref, f_ref):
    x = x_ref[0]
    h = x * lax.rsqrt(jnp.mean(x * x, axis=-1, keepdims=True) + EPS) * a_ref[0] + s_ref[0]
    hb = h.astype(BF16)
    aw = ATTN_WIDTH
    q = jnp.dot(hb, w_ref[:, 0:aw], preferred_element_type=F32)
    q_ref[0] = (_head_rms(q, qg_ref[...]) * (HEAD_DIM ** -0.5)).astype(BF16)
    k = jnp.dot(hb, w_ref[:, aw:2 * aw], preferred_element_type=F32)
    k_ref[0] = _head_rms(k, kg_ref[...]).astype(BF16)
    v_ref[0] = jnp.dot(hb, w_ref[:, 2 * aw:3 * aw], preferred_element_type=F32).astype(BF16)
    a = jnp.dot(hb, w_ref[:, 3 * aw:3 * aw + CONV_WIDTH], preferred_element_type=F32)
    g = jnp.dot(hb, w_ref[:, 3 * aw + CONV_WIDTH:], preferred_element_type=F32)
    u_ref[0] = (a * jax.nn.sigmoid(g)).astype(BF16)
    fl = jnp.dot(hb, wf_ref[...], preferred_element_type=F32) + bf_ref[...]
    f_ref[0] = _log_sigmoid(fl)


def _in_proj(x, a1, s1, w_cat, w_f, b_f, q_g, k_g):
    b, s, d = x.shape
    tm = TM_PROJ
    ncat = w_cat.shape[1]
    tok = lambda w: pl.BlockSpec((1, tm, w), lambda bi, i: (bi, i, 0))
    per_batch = pl.BlockSpec((1, 1, d), lambda bi, i: (bi, 0, 0))
    return pl.pallas_call(
        _inproj_kernel,
        grid=(b, s // tm),
        in_specs=[tok(d), per_batch, per_batch,
                  _resident((d, ncat)), _resident((d, LANES)), _resident((1, LANES)),
                  _resident((1, HEAD_DIM)), _resident((1, HEAD_DIM))],
        out_specs=[tok(ATTN_WIDTH), tok(ATTN_WIDTH), tok(ATTN_WIDTH), tok(CONV_WIDTH), tok(LANES)],
        out_shape=[jax.ShapeDtypeStruct((b, s, ATTN_WIDTH), BF16)] * 3
        + [jax.ShapeDtypeStruct((b, s, CONV_WIDTH), BF16),
           jax.ShapeDtypeStruct((b, s, LANES), F32)],
        compiler_params=_params(2),
        name="in_proj",
    )(x, a1, s1, w_cat, w_f, b_f, q_g, k_g)


def _bf16_pieces(c):
    p0 = c.astype(BF16)
    r0 = c - p0.astype(F32)
    p1 = r0.astype(BF16)
    p2 = (r0 - p1.astype(F32)).astype(BF16)
    return p0, p1, p2


def _cumsum_kernel(f_ref, c_ref):
    ch = CUM_CHUNK
    s = f_ref.shape[1]
    row = lax.broadcasted_iota(I32, (ch, ch), 0)
    col = lax.broadcasted_iota(I32, (ch, ch), 1)
    tri = (col <= row).astype(BF16)
    carry = jnp.zeros((1, LANES), F32)
    for i in range(s // ch):
        cs = carry
        for p in _bf16_pieces(f_ref[0, i * ch:(i + 1) * ch, :]):
            cs = cs + jnp.dot(tri, p, preferred_element_type=F32)
        c_ref[0, i * ch:(i + 1) * ch, :] = cs
        carry = cs[ch - 1:ch, :]


def _forget_cumsum(logf):
    b, s, _ = logf.shape
    return pl.pallas_call(
        _cumsum_kernel,
        grid=(b,),
        in_specs=[pl.BlockSpec((1, s, LANES), lambda bi: (bi, 0, 0))],
        out_specs=pl.BlockSpec((1, s, LANES), lambda bi: (bi, 0, 0)),
        out_shape=jax.ShapeDtypeStruct((b, s, LANES), F32),
        compiler_params=_params(1),
        name="forget_cumsum",
    )(logf)


def _attn_kernel(q_ref, k_ref, v_ref, cq_ref, ck_ref, o_ref, kx_sc, s_sc, m_sc, l_sc, acc_sc):
    h = pl.program_id(1)
    qi = pl.program_id(2)
    tq = TQ
    n_kv = k_ref.shape[1] // tq
    sel_r = lax.broadcasted_iota(I32, (LANES, LANES), 0)
    sel_c = lax.broadcasted_iota(I32, (LANES, LANES), 1)
    lane1 = lax.broadcasted_iota(I32, (1, LANES), 1)

    def spread(c, first_lane, sign):
        out = jnp.zeros(c.shape, F32)
        for i, p in enumerate(_bf16_pieces(c)):
            sel = jnp.where((sel_r == h) & (sel_c == first_lane + i), sign, 0.0).astype(BF16)
            out = out + jnp.dot(p, sel, preferred_element_type=F32)
        return out

    @pl.when(qi == 0)
    def _():
        ones_k = jnp.where((lane1 >= 3) & (lane1 < 6), 1.0, 0.0)

        def fill(i, carry):
            st = pl.multiple_of(i * tq, tq)
            kx_sc[pl.ds(st, tq), :] = (spread(ck_ref[0, pl.ds(st, tq), :], 0, -1.0) + ones_k).astype(BF16)
            return carry

        lax.fori_loop(0, n_kv, fill, 0)

    ones_q = jnp.where(lane1 < 3, 1.0, 0.0)
    qx = (spread(cq_ref[0], 3, 1.0) + ones_q).astype(BF16)
    q_aug = jnp.concatenate([q_ref[0], qx], axis=-1)

    m_sc[...] = jnp.full(m_sc.shape, -jnp.inf, F32)
    l_sc[...] = jnp.zeros(l_sc.shape, F32)
    acc_sc[...] = jnp.zeros(acc_sc.shape, F32)

    def scores_into(j, slot):
        st = pl.multiple_of(j * tq, tq)
        k_aug = jnp.concatenate([k_ref[0, pl.ds(st, tq), :], kx_sc[pl.ds(st, tq), :]], axis=-1)
        s_sc[slot] = lax.dot_general(k_aug, q_aug, (((1,), (1,)), ((), ())),
                                     preferred_element_type=F32)

    def update(j, slot, masked):
        st = pl.multiple_of(j * tq, tq)
        s = s_sc[slot]
        if masked:
            key = lax.broadcasted_iota(I32, (tq, tq), 0)
            qry = lax.broadcasted_iota(I32, (tq, tq), 1)
            s = jnp.where(key <= qry, s, -jnp.inf)
        m_prev = m_sc[...]
        m_new = jnp.maximum(m_prev, jnp.max(s, axis=0, keepdims=True))
        alpha = jnp.exp(m_prev - m_new)
        p = jnp.exp(s - m_new)
        l_sc[...] = alpha * l_sc[...] + jnp.sum(p, axis=0, keepdims=True)
        pv = lax.dot_general(v_ref[0, pl.ds(st, tq), :], p.astype(BF16), (((0,), (0,)), ((), ())),
                             preferred_element_type=F32)
        acc_sc[...] = alpha * acc_sc[...] + pv
        m_sc[...] = m_new

    scores_into(0, 0)

    def body(jj, carry):
        j = 2 * jj
        scores_into(j + 1, 1)
        update(j, 0, False)
        scores_into(j + 2, 0)
        update(j + 1, 1, False)
        return carry

    lax.fori_loop(0, qi // 2, body, 0)

    @pl.when(qi % 2 == 1)
    def _():
        scores_into(qi, 1)
        update(qi - 1, 0, False)
        update(qi, 1, True)

    @pl.when(qi % 2 == 0)
    def _():
        update(qi, 0, True)

    o_ref[0] = (acc_sc[...] / l_sc[...]).T.astype(BF16)


def _fox_attention(q, k, v, cum):
    b, s, _ = q.shape
    tq = TQ
    return pl.pallas_call(
        _attn_kernel,
        grid=(b, ATTN_HEADS, s // tq),
        in_specs=[pl.BlockSpec((1, tq, HEAD_DIM), lambda bi, h, i: (bi, i, h)),
                  pl.BlockSpec((1, s, HEAD_DIM), lambda bi, h, i: (bi, 0, h)),
                  pl.BlockSpec((1, s, HEAD_DIM), lambda bi, h, i: (bi, 0, h)),
                  pl.BlockSpec((1, tq, LANES), lambda bi, h, i: (bi, i, 0)),
                  pl.BlockSpec((1, s, LANES), lambda bi, h, i: (bi, 0, 0))],
        out_specs=pl.BlockSpec((1, tq, HEAD_DIM), lambda bi, h, i: (bi, i, h)),
        out_shape=jax.ShapeDtypeStruct((b, s, ATTN_WIDTH), BF16),
        scratch_shapes=[pltpu.VMEM((s, LANES), BF16),
                        pltpu.VMEM((2, tq, tq), F32),
                        pltpu.VMEM((1, tq), F32), pltpu.VMEM((1, tq), F32),
                        pltpu.VMEM((HEAD_DIM, tq), F32)],
        compiler_params=_params(3),
        name="fox_attention",
    )(q, k, v, cum, cum)


def _conv_kernel(u_ref, w_ref, cb_ref, lg_ref, lb_ref, o_ref, ubuf, shifted, wb):
    i = pl.program_id(1)
    ts, halo, ck = TS_CONV, CONV_HALO, CONV_CHUNK

    @pl.when(i == 0)
    def _():
        ubuf[0:halo, :] = jnp.zeros((halo, CONV_WIDTH), F32)
        for j in range(CONV_KERNEL):
            wb[j * SUBLANES:(j + 1) * SUBLANES, :] = jnp.broadcast_to(
                w_ref[j:j + 1, :], (SUBLANES, CONV_WIDTH))

    ubuf[halo:halo + ts, :] = u_ref[0].astype(F32)
    base = halo - (CONV_KERNEL - 1)
    span = shifted.shape[1]
    for r in range(1, SUBLANES):
        shifted[r] = ubuf[r:r + span, :]
    for c in range(ts // ck):
        acc = jnp.zeros((ck, CONV_WIDTH), F32)
        for j in range(CONV_KERNEL):
            off = c * ck + base + j
            r, al = off % SUBLANES, off - off % SUBLANES
            slab = ubuf[al:al + ck, :] if r == 0 else shifted[r, al:al + ck, :]
            wj = wb[j * SUBLANES:(j + 1) * SUBLANES, :]
            acc = acc + slab * jnp.concatenate([wj] * (ck // SUBLANES), axis=0)
        y = acc + cb_ref[...]
        mu = jnp.mean(y, axis=-1, keepdims=True)
        yc = y - mu
        var = jnp.mean(yc * yc, axis=-1, keepdims=True)
        z = yc * lax.rsqrt(var + EPS) * lg_ref[...] + lb_ref[...]
        o_ref[0, c * ck:(c + 1) * ck, :] = (z * jax.nn.sigmoid(z)).astype(BF16)
    ubuf[0:halo, :] = ubuf[ts:ts + halo, :]


def _conv_module(u, conv_w, conv_b, ln_g, ln_b):
    b, s, cw = u.shape
    ts = TS_CONV
    return pl.pallas_call(
        _conv_kernel,
        grid=(b, s // ts),
        in_specs=[pl.BlockSpec((1, ts, cw), lambda bi, i: (bi, i, 0)),
                  _resident((CONV_KERNEL, cw)), _resident((1, cw)),
                  _resident((1, cw)), _resident((1, cw))],
        out_specs=pl.BlockSpec((1, ts, cw), lambda bi, i: (bi, i, 0)),
        out_shape=jax.ShapeDtypeStruct((b, s, cw), BF16),
        scratch_shapes=[pltpu.VMEM((CONV_HALO + ts, cw), F32),
                        pltpu.VMEM((SUBLANES, ts + CONV_HALO - SUBLANES, cw), F32),
                        pltpu.VMEM((CONV_KERNEL * SUBLANES, cw), F32)],
        compiler_params=_params(2),
        name="conv_module",
    )(u, conv_w, conv_b, ln_g, ln_b)


def _store_row_slabs(ref, val):
    rows = val.shape[0]
    for s in range(ROW_SLAB):
        ref[pl.ds(s, rows, stride=ROW_SLAB), :] = val[:, s * LANES:(s + 1) * LANES]


def _load_row_slabs(ref, start, rows):
    parts = [ref[pl.ds(start * ROW_SLAB + s, rows, stride=ROW_SLAB), :] for s in range(ROW_SLAB)]
    return jnp.concatenate(parts, axis=-1)


def _outproj_kernel(at_ref, cv_ref, x_ref, wo_ref, g1_ref, a2_ref, s2_ref, wrh_ref, wrl_ref, br_ref,
                    x1_ref, z_ref, lg_ref):
    mix = (jnp.dot(at_ref[0], wo_ref[0:ATTN_WIDTH, :], preferred_element_type=F32)
           + jnp.dot(cv_ref[0], wo_ref[ATTN_WIDTH:, :], preferred_element_type=F32))
    x1 = x_ref[0] + g1_ref[0] * mix
    x1_ref[0] = x1
    h2 = x1 * lax.rsqrt(jnp.mean(x1 * x1, axis=-1, keepdims=True) + EPS) * a2_ref[0] + s2_ref[0]
    _store_row_slabs(z_ref, h2)
    hi = h2.astype(BF16)
    lo = (h2 - hi.astype(F32)).astype(BF16)
    lg_ref[0] = (jnp.dot(hi, wrh_ref[...], preferred_element_type=F32)
                 + jnp.dot(lo, wrh_ref[...], preferred_element_type=F32)
                 + jnp.dot(hi, wrl_ref[...], preferred_element_type=F32)) + br_ref[...]


def _out_proj(attn, conv, x, w_out, g1, a2, s2, wr_hi, wr_lo, b_r):
    b, s, d = x.shape
    tm = TM_PROJ
    nt = s // tm
    tok = lambda w: pl.BlockSpec((1, tm, w), lambda bi, i: (bi, i, 0))
    per_batch = pl.BlockSpec((1, 1, d), lambda bi, i: (bi, 0, 0))
    return pl.pallas_call(
        _outproj_kernel,
        grid=(b, nt),
        in_specs=[tok(ATTN_WIDTH), tok(CONV_WIDTH), tok(d), _resident((d, d)),
                  per_batch, per_batch, per_batch,
                  _resident((d, LANES)), _resident((d, LANES)), _resident((1, LANES))],
        out_specs=[tok(d),
                   pl.BlockSpec((tm * ROW_SLAB, LANES), lambda bi, i: (bi * nt + i, 0)),
                   tok(LANES)],
        out_shape=[jax.ShapeDtypeStruct((b, s, d), F32),
                   jax.ShapeDtypeStruct((b * s * ROW_SLAB, LANES), F32),
                   jax.ShapeDtypeStruct((b, s, LANES), F32)],
        compiler_params=_params(2),
        name="out_proj",
    )(attn, conv, x, w_out, g1, a2, s2, wr_hi, wr_lo, b_r)


def _route_kernel(lg_ref, ri_ref, rw_ref, cnt_ref, carry):
    i = pl.program_id(0)
    tm = TM_ROUTE

    @pl.when(i == 0)
    def _():
        carry[...] = jnp.zeros(carry.shape, F32)

    l = lg_ref[...]
    lane = lax.broadcasted_iota(I32, (tm, LANES), 1).astype(F32)
    vals, idxs = [], []
    for _ in range(TOP_K):
        m = jnp.max(l, axis=-1, keepdims=True)
        ix = jnp.min(jnp.where(l == m, lane, float(LANES)), axis=-1, keepdims=True)
        vals.append(m)
        idxs.append(ix)
        l = jnp.where(lane == ix, -jnp.inf, l)
    es = [jnp.exp(v - vals[0]) for v in vals]
    den = es[0] + es[1] + es[2] + es[3]
    onehot = jnp.zeros((tm, LANES), F32)
    for ix in idxs:
        onehot = onehot + jnp.where(lane == ix, 1.0, 0.0)
    row = lax.broadcasted_iota(I32, (tm, tm), 0)
    col = lax.broadcasted_iota(I32, (tm, tm), 1)
    strict = (col < row).astype(BF16)
    before = jnp.dot(strict, onehot.astype(BF16), preferred_element_type=F32) + carry[0:1, :]
    ri = jnp.zeros((tm, LANES), F32)
    rw = jnp.zeros((tm, LANES), F32)
    for k in range(TOP_K):
        rank = jnp.sum(jnp.where(lane == idxs[k], before, 0.0), axis=-1, keepdims=True)
        ri = jnp.where(lane == k, idxs[k], ri)
        ri = jnp.where(lane == TOP_K + k, rank, ri)
        rw = jnp.where(lane == k, es[k] / den, rw)
    ri_ref[...] = ri.astype(I32)
    rw_ref[...] = rw
    carry[0:1, :] = carry[0:1, :] + jnp.sum(onehot, axis=0, keepdims=True)
    cnt_ref[...] = carry[...]


def _route(logits):
    t = logits.shape[0]
    tm = TM_ROUTE
    return pl.pallas_call(
        _route_kernel,
        grid=(t // tm,),
        in_specs=[pl.BlockSpec((tm, LANES), lambda i: (i, 0))],
        out_specs=[pl.BlockSpec((tm, LANES), lambda i: (i, 0)),
                   pl.BlockSpec((tm, LANES), lambda i: (i, 0)),
                   pl.BlockSpec((SUBLANES, LANES), lambda i: (0, 0))],
        out_shape=[jax.ShapeDtypeStruct((t, LANES), I32),
                   jax.ShapeDtypeStruct((t, LANES), F32),
                   jax.ShapeDtypeStruct((SUBLANES, LANES), F32)],
        scratch_shapes=[pltpu.VMEM((SUBLANES, LANES), F32)],
        compiler_params=_params(1),
        name="route",
    )(logits)


def _issue_row_gather(idx_ref, idx_base, n_rows, src_hbm, dst_buf, sem):
    def body(g, carry):
        for u in range(GATHER_UNROLL):
            r = g * GATHER_UNROLL + u
            t = idx_ref[idx_base + r]
            pltpu.make_async_copy(
                src_hbm.at[pl.ds(pl.multiple_of(t * ROW_SLAB, ROW_SLAB), ROW_SLAB), :],
                dst_buf.at[pl.ds(pl.multiple_of(r * ROW_SLAB, ROW_SLAB), ROW_SLAB), :],
                sem).start()
        return carry
    lax.fori_loop(0, n_rows // GATHER_UNROLL, body, 0)


def _wait_row_gather(n_rows, src_hbm, dst_buf, sem):
    pltpu.make_async_copy(src_hbm.at[pl.ds(0, n_rows * ROW_SLAB), :], dst_buf, sem).wait()


def _moe_up_kernel(be_ref, tok_ref, nu_ref, z_hbm, wg_ref, bg_ref, wu_ref, bu_ref, hid_ref, zbuf, sem):
    b = pl.program_id(0)
    n_used = nu_ref[0]
    tm = TM_MOE

    @pl.when(b == 0)
    def _():
        _issue_row_gather(tok_ref, 0, tm, z_hbm, zbuf.at[0], sem.at[0])

    @pl.when(b + 1 < n_used)
    def _():
        nxt = (b + 1) % 2
        _issue_row_gather(tok_ref, (b + 1) * tm, tm, z_hbm, zbuf.at[nxt], sem.at[nxt])

    @pl.when(b < n_used)
    def _():
        cur = b % 2
        _wait_row_gather(tm, z_hbm, zbuf.at[cur], sem.at[cur])
        x = _load_row_slabs(zbuf.at[cur], 0, tm).astype(BF16)
        g = jnp.dot(x, wg_ref[0], preferred_element_type=F32) + bg_ref[0]
        u = jnp.dot(x, wu_ref[0], preferred_element_type=F32) + bu_ref[0]
        g = jnp.minimum(g, SWIGLU_LIMIT)
        u = jnp.clip(u, -SWIGLU_LIMIT, SWIGLU_LIMIT)
        hid_ref[...] = ((u + 1.0) * (g * jax.nn.sigmoid(SWIGLU_ALPHA * g))).astype(BF16)

    @pl.when(b >= n_used)
    def _():
        hid_ref[...] = jnp.zeros(hid_ref.shape, BF16)


def _moe_up(block_e, slot_tok, n_used, z, w_gate, b_gate, w_up, b_up):
    e, d, f = w_gate.shape
    n_blocks = block_e.shape[0]
    tm = TM_MOE
    wspec = pl.BlockSpec((1, d, f), lambda b, be, tok, nu: (be[b], 0, 0))
    bspec = pl.BlockSpec((1, 1, f), lambda b, be, tok, nu: (be[b], 0, 0))
    return pl.pallas_call(
        _moe_up_kernel,
        grid_spec=pltpu.PrefetchScalarGridSpec(
            num_scalar_prefetch=3,
            grid=(n_blocks,),
            in_specs=[pl.BlockSpec(memory_space=pl.ANY), wspec, bspec, wspec, bspec],
            out_specs=pl.BlockSpec((tm, f), lambda b, be, tok, nu: (b, 0)),
            scratch_shapes=[pltpu.VMEM((2, tm * ROW_SLAB, LANES), F32),
                            pltpu.SemaphoreType.DMA((2,))]),
        out_shape=jax.ShapeDtypeStruct((n_blocks * tm, f), BF16),
        compiler_params=_params(1),
        name="moe_up",
    )(block_e, slot_tok, n_used, z, w_gate, b_gate, w_up, b_up)


def _moe_down_kernel(be_ref, nu_ref, hid_ref, wd_ref, bd_ref, ys_ref):
    b = pl.program_id(0)

    @pl.when(b < nu_ref[0])
    def _():
        out = jnp.dot(hid_ref[...], wd_ref[0], preferred_element_type=F32) + bd_ref[0]
        _store_row_slabs(ys_ref, out)

    @pl.when(b >= nu_ref[0])
    def _():
        ys_ref[...] = jnp.zeros(ys_ref.shape, F32)


def _moe_down(block_e, n_used, hid, w_down, b_down):
    e, f, d = w_down.shape
    n_blocks = block_e.shape[0]
    tm = TM_MOE
    return pl.pallas_call(
        _moe_down_kernel,
        grid_spec=pltpu.PrefetchScalarGridSpec(
            num_scalar_prefetch=2,
            grid=(n_blocks,),
            in_specs=[pl.BlockSpec((tm, f), lambda b, be, nu: (b, 0)),
                      pl.BlockSpec((1, f, d), lambda b, be, nu: (be[b], 0, 0)),
                      pl.BlockSpec((1, 1, d), lambda b, be, nu: (be[b], 0, 0))],
            out_specs=pl.BlockSpec((tm * ROW_SLAB, LANES), lambda b, be, nu: (b, 0))),
        out_shape=jax.ShapeDtypeStruct((n_blocks * tm * ROW_SLAB, LANES), F32),
        compiler_params=_params(1),
        name="moe_down",
    )(block_e, n_used, hid, w_down, b_down)


def _combine_kernel(dest_ref, ys_hbm, x1_ref, rw_ref, g2_ref, o_ref, buf, sem):
    i = pl.program_id(0)
    n = pl.num_programs(0)
    tm = TM_COMB
    rows = TOP_K * tm

    @pl.when(i == 0)
    def _():
        _issue_row_gather(dest_ref, 0, rows, ys_hbm, buf.at[0], sem.at[0])

    @pl.when(i + 1 < n)
    def _():
        nxt = (i + 1) % 2
        _issue_row_gather(dest_ref, (i + 1) * rows, rows, ys_hbm, buf.at[nxt], sem.at[nxt])

    cur = i % 2
    _wait_row_gather(rows, ys_hbm, buf.at[cur], sem.at[cur])
    rw = rw_ref[...]
    y = jnp.zeros((tm, D_MODEL), F32)
    for k in range(TOP_K):
        y = y + rw[:, k:k + 1] * _load_row_slabs(buf.at[cur], k * tm, tm)
    o_ref[...] = x1_ref[...] + g2_ref[0] * y


def _combine(dest_km, ys, x1, rw, g2, seq):
    t, d = x1.shape
    tm = TM_COMB
    per_seq = seq // tm
    return pl.pallas_call(
        _combine_kernel,
        grid_spec=pltpu.PrefetchScalarGridSpec(
            num_scalar_prefetch=1,
            grid=(t // tm,),
            in_specs=[pl.BlockSpec(memory_space=pl.ANY),
                      pl.BlockSpec((tm, d), lambda i, dr: (i, 0)),
                      pl.BlockSpec((tm, LANES), lambda i, dr: (i, 0)),
                      pl.BlockSpec((1, 1, d), lambda i, dr: (i // per_seq, 0, 0))],
            out_specs=pl.BlockSpec((tm, d), lambda i, dr: (i, 0)),
            scratch_shapes=[pltpu.VMEM((2, TOP_K * tm * ROW_SLAB, LANES), F32),
                            pltpu.SemaphoreType.DMA((2,))]),
        out_shape=jax.ShapeDtypeStruct((t, d), F32),
        compiler_params=_params(1),
        name="combine",
    )(dest_km, ys, x1, rw, g2)


def _pad_cols(w, n):
    return jnp.pad(w, ((0, 0), (0, n - w.shape[1])))


def _layer(x, mod, norm_mix_g, norm_ffn_g, w_in, b_f, q_norm_g, k_norm_g, conv_w, conv_b,
           conv_ln_g, conv_ln_b, w_out, w_router, b_router, w_gate, b_gate, w_up, b_up,
           w_down, b_down):
    b, s, d = x.shape
    t = b * s
    shift1, scale1, gate1, shift2, scale2, gate2 = [m[:, None, :] for m in jnp.split(mod, 6, axis=-1)]
    a1 = norm_mix_g[None, None, :] * (1.0 + scale1)
    a2 = norm_ffn_g[None, None, :] * (1.0 + scale2)

    aw = ATTN_WIDTH
    w_cat = jnp.concatenate([w_in[:, :3 * aw], w_in[:, 3 * aw + ATTN_HEADS:]], axis=1).astype(BF16)
    w_f = _pad_cols(w_in[:, 3 * aw:3 * aw + ATTN_HEADS], LANES).astype(BF16)
    b_f_pad = _pad_cols(b_f[None, :], LANES)

    q, k, v, u, logf = _in_proj(x, a1, shift1, w_cat, w_f, b_f_pad,
                                q_norm_g[None, :], k_norm_g[None, :])
    attn = _fox_attention(q, k, v, _forget_cumsum(logf))
    conv = _conv_module(u, conv_w, conv_b[None, :], conv_ln_g[None, :], conv_ln_b[None, :])

    wr = _pad_cols(w_router, LANES)
    wr_hi = wr.astype(BF16)
    wr_lo = (wr - wr_hi.astype(F32)).astype(BF16)
    b_r = jnp.concatenate([b_router, jnp.full((LANES - N_EXPERTS,), NEG_BIG, F32)])[None, :]
    x1, z, logits = _out_proj(attn, conv, x, w_out.astype(BF16), gate1, a2, shift2, wr_hi, wr_lo, b_r)

    ri, rw, cnt = _route(logits.reshape(t, LANES))

    tm = TM_MOE
    n_blocks = t * TOP_K // tm + N_EXPERTS
    idx = ri[:, 0:TOP_K]
    rank = ri[:, TOP_K:2 * TOP_K]
    counts = cnt[0, :N_EXPERTS].astype(I32)
    padded = (counts + tm - 1) // tm * tm
    padded_end = jnp.cumsum(padded)
    padded_start = padded_end - padded
    dest = padded_start[idx] + rank
    slot_tok = jnp.zeros((n_blocks * tm,), I32).at[dest.reshape(-1)].set(
        jnp.repeat(jnp.arange(t, dtype=I32), TOP_K), unique_indices=True, mode="promise_in_bounds")
    block_start = jnp.arange(n_blocks, dtype=I32)[:, None] * tm
    block_e = jnp.minimum(jnp.sum((padded_end[None, :] <= block_start).astype(I32), axis=1), N_EXPERTS - 1)
    n_used = (padded_end[-1:] // tm).astype(I32)
    dest_km = dest.reshape(t // TM_COMB, TM_COMB, TOP_K).transpose(0, 2, 1).reshape(-1)

    hid = _moe_up(block_e, slot_tok, n_used, z, w_gate.astype(BF16), b_gate[:, None, :],
                  w_up.astype(BF16), b_up[:, None, :])
    ys = _moe_down(block_e, n_used, hid, w_down.astype(BF16), b_down[:, None, :])
    out = _combine(dest_km, ys, x1.reshape(t, d), rw, gate2, s)
    return out.reshape(b, s, d)


def kernel(x, c, ada_w, ada_b, norm_mix_g, norm_ffn_g, w_in, b_f, q_norm_g, k_norm_g, conv_w, conv_b,
           conv_ln_g, conv_ln_b, w_out, w_router, b_router, w_gate, b_gate, w_up, b_up, w_down, b_down):
    b = x.shape[0]
    c_pad = jnp.pad(c, ((0, SUBLANES - b), (0, 0)))
    for l in range(ada_w.shape[0]):
        mod = _ada_mod(c_pad, ada_w[l], ada_b[l])[:b]
        x = _layer(x, mod, norm_mix_g[l], norm_ffn_g[l], w_in[l], b_f[l], q_norm_g[l], k_norm_g[l],
                   conv_w[l], conv_b[l], conv_ln_g[l], conv_ln_b[l], w_out[l], w_router[l],
                   b_router[l], w_gate[l], b_gate[l], w_up[l], b_up[l], w_down[l], b_down[l])
    return x
```

```python
import functools

import jax
import jax.numpy as jnp
from jax import lax
from jax.experimental import pallas as pl
from jax.experimental.pallas import tpu as pltpu

F32 = jnp.float32
BF16 = jnp.bfloat16
I32 = jnp.int32
U32 = jnp.uint32

D_MODEL = 2048
ATTN_HEADS = 8
HEAD_DIM = 128
ATTN_WIDTH = ATTN_HEADS * HEAD_DIM
CONV_WIDTH = D_MODEL - ATTN_WIDTH
CONV_KERNEL = 31
N_EXPERTS = 32
TOP_K = 4
SWIGLU_LIMIT = 7.0
SWIGLU_ALPHA = 1.702
EPS = 1e-6

LANES = 128
SUBLANES = 8
HALF = D_MODEL // 2
ROW_SLAB = HALF // LANES
VMEM_LIMIT = 56 * 1024 * 1024

TM_PROJ = 512
TQ = 512
CUM_CHUNK = 256
TS_CONV = 256
CONV_HALO = 32
CONV_CHUNK = 32
TM_ROUTE = 512
TM_MOE = 256
TM_DISP = 256
TM_COMB = 256
GATHER_UNROLL = 32
NEG_BIG = -1e30


def _params(n_axes):
    return pltpu.CompilerParams(
        dimension_semantics=("arbitrary",) * n_axes, vmem_limit_bytes=VMEM_LIMIT)


def _resident(shape):
    nd = len(shape)
    return pl.BlockSpec(shape, lambda *_: (0,) * nd, pipeline_mode=pl.Buffered(1))


def _ada_kernel(c_ref, w_ref, b_ref, o_ref):
    c = c_ref[...]
    c_act = (c * jax.nn.sigmoid(c)).astype(BF16)
    o_ref[...] = jnp.dot(c_act, w_ref[...].astype(BF16), preferred_element_type=F32) + b_ref[...]


def _ada_mod(c_pad, ada_w, ada_b):
    rows, d = c_pad.shape
    n = ada_w.shape[1]
    tn = 1024
    return pl.pallas_call(
        _ada_kernel,
        grid=(n // tn,),
        in_specs=[pl.BlockSpec((rows, d), lambda j: (0, 0)),
                  pl.BlockSpec((d, tn), lambda j: (0, j)),
                  pl.BlockSpec((1, tn), lambda j: (0, j))],
        out_specs=pl.BlockSpec((rows, tn), lambda j: (0, j)),
        out_shape=jax.ShapeDtypeStruct((rows, n), F32),
        compiler_params=_params(1),
        name="ada_mod",
    )(c_pad, ada_w, ada_b.reshape(1, n))


def _log_sigmoid(x):
    return jnp.minimum(x, 0.0) - jnp.log1p(jnp.exp(-jnp.abs(x)))


def _head_rms(y, g):
    outs = []
    for h in range(ATTN_HEADS):
        yh = y[:, h * HEAD_DIM:(h + 1) * HEAD_DIM]
        r = lax.rsqrt(jnp.mean(yh * yh, axis=-1, keepdims=True) + EPS)
        outs.append(yh * r * g)
    return jnp.concatenate(outs, axis=-1)


def _inproj_kernel(x_ref, a_ref, s_ref, w_ref, wf_ref, bf_ref, qg_ref, kg_ref,
                   q_ref, k_ref, v_ref, u_ref, f_ref):
    x = x_ref[0]
    h = x * lax.rsqrt(jnp.mean(x * x, axis=-1, keepdims=True) + EPS) * a_ref[0] + s_ref[0]
    hb = h.astype(BF16)
    aw = ATTN_WIDTH
    q = jnp.dot(hb, w_ref[:, 0:aw], preferred_element_type=F32)
    q_ref[0] = (_head_rms(q, qg_ref[...]) * (HEAD_DIM ** -0.5)).astype(BF16)
    k = jnp.dot(hb, w_ref[:, aw:2 * aw], preferred_element_type=F32)
    k_ref[0] = _head_rms(k, kg_ref[...]).astype(BF16)
    v_ref[0] = jnp.dot(hb, w_ref[:, 2 * aw:3 * aw], preferred_element_type=F32).astype(BF16)
    a = jnp.dot(hb, w_ref[:, 3 * aw:3 * aw + CONV_WIDTH], preferred_element_type=F32)
    g = jnp.dot(hb, w_ref[:, 3 * aw + CONV_WIDTH:], preferred_element_type=F32)
    u_ref[0] = (a * jax.nn.sigmoid(g)).astype(BF16)
    fl = jnp.dot(hb, wf_ref[...], preferred_element_type=F32) + bf_ref[...]
    f_ref[0] = _log_sigmoid(fl)


def _in_proj(x, a1, s1, w_cat, w_f, b_f, q_g, k_g):
    b, s, d = x.shape
    tm = TM_PROJ
    ncat = w_cat.shape[1]
    tok = lambda w: pl.BlockSpec((1, tm, w), lambda bi, i: (bi, i, 0))
    per_batch = pl.BlockSpec((1, 1, d), lambda bi, i: (bi, 0, 0))
    return pl.pallas_call(
        _inproj_kernel,
        grid=(b, s // tm),
        in_specs=[tok(d), per_batch, per_batch,
                  _resident((d, ncat)), _resident((d, LANES)), _resident((1, LANES)),
                  _resident((1, HEAD_DIM)), _resident((1, HEAD_DIM))],
        out_specs=[tok(ATTN_WIDTH), tok(ATTN_WIDTH), tok(ATTN_WIDTH), tok(CONV_WIDTH), tok(LANES)],
        out_shape=[jax.ShapeDtypeStruct((b, s, ATTN_WIDTH), BF16)] * 3
        + [jax.ShapeDtypeStruct((b, s, CONV_WIDTH), BF16),
           jax.ShapeDtypeStruct((b, s, LANES), F32)],
        compiler_params=_params(2),
        name="in_proj",
    )(x, a1, s1, w_cat, w_f, b_f, q_g, k_g)


def _bf16_pieces(c):
    p0 = c.astype(BF16)
    r0 = c - p0.astype(F32)
    p1 = r0.astype(BF16)
    p2 = (r0 - p1.astype(F32)).astype(BF16)
    return p0, p1, p2


def _cumsum_kernel(f_ref, c_ref):
    ch = CUM_CHUNK
    s = f_ref.shape[1]
    row = lax.broadcasted_iota(I32, (ch, ch), 0)
    col = lax.broadcasted_iota(I32, (ch, ch), 1)
    tri = (col <= row).astype(BF16)
    carry = jnp.zeros((1, LANES), F32)
    for i in range(s // ch):
        cs = carry
        for p in _bf16_pieces(f_ref[0, i * ch:(i + 1) * ch, :]):
            cs = cs + jnp.dot(tri, p, preferred_element_type=F32)
        c_ref[0, i * ch:(i + 1) * ch, :] = cs
        carry = cs[ch - 1:ch, :]


def _forget_cumsum(logf):
    b, s, _ = logf.shape
    return pl.pallas_call(
        _cumsum_kernel,
        grid=(b,),
        in_specs=[pl.BlockSpec((1, s, LANES), lambda bi: (bi, 0, 0))],
        out_specs=pl.BlockSpec((1, s, LANES), lambda bi: (bi, 0, 0)),
        out_shape=jax.ShapeDtypeStruct((b, s, LANES), F32),
        compiler_params=_params(1),
        name="forget_cumsum",
    )(logf)


def _attn_kernel(q_ref, k_ref, v_ref, cq_ref, ck_ref, o_ref, kx_sc, s_sc, m_sc, l_sc, acc_sc):
    h = pl.program_id(1)
    qi = pl.program_id(2)
    tq = TQ
    n_kv = k_ref.shape[1] // tq
    sel_r = lax.broadcasted_iota(I32, (LANES, LANES), 0)
    sel_c = lax.broadcasted_iota(I32, (LANES, LANES), 1)
    lane1 = lax.broadcasted_iota(I32, (1, LANES), 1)

    def spread(c, first_lane, sign):
        out = jnp.zeros(c.shape, F32)
        for i, p in enumerate(_bf16_pieces(c)):
            sel = jnp.where((sel_r == h) & (sel_c == first_lane + i), sign, 0.0).astype(BF16)
            out = out + jnp.dot(p, sel, preferred_element_type=F32)
        return out

    @pl.when(qi == 0)
    def _():
        ones_k = jnp.where((lane1 >= 3) & (lane1 < 6), 1.0, 0.0)

        def fill(i, carry):
            st = pl.multiple_of(i * tq, tq)
            kx_sc[pl.ds(st, tq), :] = (spread(ck_ref[0, pl.ds(st, tq), :], 0, -1.0) + ones_k).astype(BF16)
            return carry

        lax.fori_loop(0, n_kv, fill, 0)

    ones_q = jnp.where(lane1 < 3, 1.0, 0.0)
    qx = (spread(cq_ref[0], 3, 1.0) + ones_q).astype(BF16)
    q_aug = jnp.concatenate([q_ref[0], qx], axis=-1)

    m_sc[...] = jnp.full(m_sc.shape, -jnp.inf, F32)
    l_sc[...] = jnp.zeros(l_sc.shape, F32)
    acc_sc[...] = jnp.zeros(acc_sc.shape, F32)

    def scores_into(j, slot):
        st = pl.multiple_of(j * tq, tq)
        k_aug = jnp.concatenate([k_ref[0, pl.ds(st, tq), :], kx_sc[pl.ds(st, tq), :]], axis=-1)
        s_sc[slot] = lax.dot_general(k_aug, q_aug, (((1,), (1,)), ((), ())),
                                     preferred_element_type=F32)

    def update(j, slot, masked):
        st = pl.multiple_of(j * tq, tq)
        s = s_sc[slot]
        if masked:
            key = lax.broadcasted_iota(I32, (tq, tq), 0)
            qry = lax.broadcasted_iota(I32, (tq, tq), 1)
            s = jnp.where(key <= qry, s, -jnp.inf)
        m_prev = m_sc[...]
        m_new = jnp.maximum(m_prev, jnp.max(s, axis=0, keepdims=True))
        alpha = jnp.exp(m_prev - m_new)
        p = jnp.exp(s - m_new)
        l_sc[...] = alpha * l_sc[...] + jnp.sum(p, axis=0, keepdims=True)
        pv = lax.dot_general(v_ref[0, pl.ds(st, tq), :], p.astype(BF16), (((0,), (0,)), ((), ())),
                             preferred_element_type=F32)
        acc_sc[...] = alpha * acc_sc[...] + pv
        m_sc[...] = m_new

    scores_into(0, 0)

    def body(jj, carry):
        j = 2 * jj
        scores_into(j + 1, 1)
        update(j, 0, False)
        scores_into(j + 2, 0)
        update(j + 1, 1, False)
        return carry

    lax.fori_loop(0, qi // 2, body, 0)

    @pl.when(qi % 2 == 1)
    def _():
        scores_into(qi, 1)
        update(qi - 1, 0, False)
        update(qi, 1, True)

    @pl.when(qi % 2 == 0)
    def _():
        update(qi, 0, True)

    o_ref[0] = (acc_sc[...] / l_sc[...]).T.astype(BF16)


def _fox_attention(q, k, v, cum):
    b, s, _ = q.shape
    tq = TQ
    return pl.pallas_call(
        _attn_kernel,
        grid=(b, ATTN_HEADS, s // tq),
        in_specs=[pl.BlockSpec((1, tq, HEAD_DIM), lambda bi, h, i: (bi, i, h)),
                  pl.BlockSpec((1, s, HEAD_DIM), lambda bi, h, i: (bi, 0, h)),
                  pl.BlockSpec((1, s, HEAD_DIM), lambda bi, h, i: (bi, 0, h)),
                  pl.BlockSpec((1, tq, LANES), lambda bi, h, i: (bi, i, 0)),
                  pl.BlockSpec((1, s, LANES), lambda bi, h, i: (bi, 0, 0))],
        out_specs=pl.BlockSpec((1, tq, HEAD_DIM), lambda bi, h, i: (bi, i, h)),
        out_shape=jax.ShapeDtypeStruct((b, s, ATTN_WIDTH), BF16),
        scratch_shapes=[pltpu.VMEM((s, LANES), BF16),
                        pltpu.VMEM((2, tq, tq), F32),
                        pltpu.VMEM((1, tq), F32), pltpu.VMEM((1, tq), F32),
                        pltpu.VMEM((HEAD_DIM, tq), F32)],
        compiler_params=_params(3),
        name="fox_attention",
    )(q, k, v, cum, cum)


def _conv_kernel(u_ref, w_ref, cb_ref, lg_ref, lb_ref, o_ref, ubuf, shifted, wb):
    i = pl.program_id(1)
    ts, halo, ck = TS_CONV, CONV_HALO, CONV_CHUNK

    @pl.when(i == 0)
    def _():
        ubuf[0:halo, :] = jnp.zeros((halo, CONV_WIDTH), F32)
        for j in range(CONV_KERNEL):
            wb[j * SUBLANES:(j + 1) * SUBLANES, :] = jnp.broadcast_to(
                w_ref[j:j + 1, :], (SUBLANES, CONV_WIDTH))

    ubuf[halo:halo + ts, :] = u_ref[0].astype(F32)
    base = halo - (CONV_KERNEL - 1)
    span = shifted.shape[1]
    for r in range(1, SUBLANES):
        shifted[r] = ubuf[r:r + span, :]
    for c in range(ts // ck):
        acc = jnp.zeros((ck, CONV_WIDTH), F32)
        for j in range(CONV_KERNEL):
            off = c * ck + base + j
            r, al = off % SUBLANES, off - off % SUBLANES
            slab = ubuf[al:al + ck, :] if r == 0 else shifted[r, al:al + ck, :]
            wj = wb[j * SUBLANES:(j + 1) * SUBLANES, :]
            acc = acc + slab * jnp.concatenate([wj] * (ck // SUBLANES), axis=0)
        y = acc + cb_ref[...]
        mu = jnp.mean(y, axis=-1, keepdims=True)
        yc = y - mu
        var = jnp.mean(yc * yc, axis=-1, keepdims=True)
        z = yc * lax.rsqrt(var + EPS) * lg_ref[...] + lb_ref[...]
        o_ref[0, c * ck:(c + 1) * ck, :] = (z * jax.nn.sigmoid(z)).astype(BF16)
    ubuf[0:halo, :] = ubuf[ts:ts + halo, :]


def _conv_module(u, conv_w, conv_b, ln_g, ln_b):
    b, s, cw = u.shape
    ts = TS_CONV
    return pl.pallas_call(
        _conv_kernel,
        grid=(b, s // ts),
        in_specs=[pl.BlockSpec((1, ts, cw), lambda bi, i: (bi, i, 0)),
                  _resident((CONV_KERNEL, cw)), _resident((1, cw)),
                  _resident((1, cw)), _resident((1, cw))],
        out_specs=pl.BlockSpec((1, ts, cw), lambda bi, i: (bi, i, 0)),
        out_shape=jax.ShapeDtypeStruct((b, s, cw), BF16),
        scratch_shapes=[pltpu.VMEM((CONV_HALO + ts, cw), F32),
                        pltpu.VMEM((SUBLANES, ts + CONV_HALO - SUBLANES, cw), F32),
                        pltpu.VMEM((CONV_KERNEL * SUBLANES, cw), F32)],
        compiler_params=_params(2),
        name="conv_module",
    )(u, conv_w, conv_b, ln_g, ln_b)


def _pack_rows(val):
    lo = lax.bitcast_convert_type(val[:, :HALF].astype(BF16).astype(F32), U32)
    hi = lax.bitcast_convert_type(val[:, HALF:].astype(BF16).astype(F32), U32)
    return hi | (lo >> 16)


def _unpack_rows(word):
    lo = lax.bitcast_convert_type(word << 16, F32)
    hi = lax.bitcast_convert_type(word & jnp.uint32(0xFFFF0000), F32)
    return lo, hi


def _store_row_slabs(ref, word):
    rows = word.shape[0]
    for s in range(ROW_SLAB):
        ref[pl.ds(s, rows, stride=ROW_SLAB), :] = word[:, s * LANES:(s + 1) * LANES]


def _load_row_slabs(ref, start, rows):
    parts = [ref[pl.ds(start * ROW_SLAB + s, rows, stride=ROW_SLAB), :] for s in range(ROW_SLAB)]
    return jnp.concatenate(parts, axis=-1)


def _outproj_kernel(at_ref, cv_ref, x_ref, wo_ref, g1_ref, a2_ref, s2_ref, wrh_ref, wrl_ref, br_ref,
                    x1_ref, z_ref, lg_ref):
    mix = (jnp.dot(at_ref[0], wo_ref[0:ATTN_WIDTH, :], preferred_element_type=F32)
           + jnp.dot(cv_ref[0], wo_ref[ATTN_WIDTH:, :], preferred_element_type=F32))
    x1 = x_ref[0] + g1_ref[0] * mix
    x1_ref[0] = x1
    h2 = x1 * lax.rsqrt(jnp.mean(x1 * x1, axis=-1, keepdims=True) + EPS) * a2_ref[0] + s2_ref[0]
    _store_row_slabs(z_ref, _pack_rows(h2))
    hi = h2.astype(BF16)
    lo = (h2 - hi.astype(F32)).astype(BF16)
    lg_ref[0] = (jnp.dot(hi, wrh_ref[...], preferred_element_type=F32)
                 + jnp.dot(lo, wrh_ref[...], preferred_element_type=F32)
                 + jnp.dot(hi, wrl_ref[...], preferred_element_type=F32)) + br_ref[...]


def _out_proj(attn, conv, x, w_out, g1, a2, s2, wr_hi, wr_lo, b_r):
    b, s, d = x.shape
    tm = TM_PROJ
    nt = s // tm
    tok = lambda w: pl.BlockSpec((1, tm, w), lambda bi, i: (bi, i, 0))
    per_batch = pl.BlockSpec((1, 1, d), lambda bi, i: (bi, 0, 0))
    return pl.pallas_call(
        _outproj_kernel,
        grid=(b, nt),
        in_specs=[tok(ATTN_WIDTH), tok(CONV_WIDTH), tok(d), _resident((d, d)),
                  per_batch, per_batch, per_batch,
                  _resident((d, LANES)), _resident((d, LANES)), _resident((1, LANES))],
        out_specs=[tok(d),
                   pl.BlockSpec((tm * ROW_SLAB, LANES), lambda bi, i: (bi * nt + i, 0)),
                   tok(LANES)],
        out_shape=[jax.ShapeDtypeStruct((b, s, d), F32),
                   jax.ShapeDtypeStruct((b * s * ROW_SLAB, LANES), U32),
                   jax.ShapeDtypeStruct((b, s, LANES), F32)],
        compiler_params=_params(2),
        name="out_proj",
    )(attn, conv, x, w_out, g1, a2, s2, wr_hi, wr_lo, b_r)


def _route_kernel(lg_ref, ri_ref, rw_ref, cnt_ref, carry):
    i = pl.program_id(0)
    tm = TM_ROUTE

    @pl.when(i == 0)
    def _():
        carry[...] = jnp.zeros(carry.shape, F32)

    l = lg_ref[...]
    lane = lax.broadcasted_iota(I32, (tm, LANES), 1).astype(F32)
    vals, idxs = [], []
    for _ in range(TOP_K):
        m = jnp.max(l, axis=-1, keepdims=True)
        ix = jnp.min(jnp.where(l == m, lane, float(LANES)), axis=-1, keepdims=True)
        vals.append(m)
        idxs.append(ix)
        l = jnp.where(lane == ix, -jnp.inf, l)
    es = [jnp.exp(v - vals[0]) for v in vals]
    den = es[0] + es[1] + es[2] + es[3]
    onehot = jnp.zeros((tm, LANES), F32)
    for ix in idxs:
        onehot = onehot + jnp.where(lane == ix, 1.0, 0.0)
    row = lax.broadcasted_iota(I32, (tm, tm), 0)
    col = lax.broadcasted_iota(I32, (tm, tm), 1)
    strict = (col < row).astype(BF16)
    before = jnp.dot(strict, onehot.astype(BF16), preferred_element_type=F32) + carry[0:1, :]
    ri = jnp.zeros((tm, LANES), F32)
    rw = jnp.zeros((tm, LANES), F32)
    for k in range(TOP_K):
        rank = jnp.sum(jnp.where(lane == idxs[k], before, 0.0), axis=-1, keepdims=True)
        ri = jnp.where(lane == k, idxs[k], ri)
        ri = jnp.where(lane == TOP_K + k, rank, ri)
        rw = jnp.where(lane == k, es[k] / den, rw)
    ri_ref[...] = ri.astype(I32)
    rw_ref[...] = rw
    carry[0:1, :] = carry[0:1, :] + jnp.sum(onehot, axis=0, keepdims=True)
    cnt_ref[...] = carry[...]


def _route(logits):
    t = logits.shape[0]
    tm = TM_ROUTE
    return pl.pallas_call(
        _route_kernel,
        grid=(t // tm,),
        in_specs=[pl.BlockSpec((tm, LANES), lambda i: (i, 0))],
        out_specs=[pl.BlockSpec((tm, LANES), lambda i: (i, 0)),
                   pl.BlockSpec((tm, LANES), lambda i: (i, 0)),
                   pl.BlockSpec((SUBLANES, LANES), lambda i: (0, 0))],
        out_shape=[jax.ShapeDtypeStruct((t, LANES), I32),
                   jax.ShapeDtypeStruct((t, LANES), F32),
                   jax.ShapeDtypeStruct((SUBLANES, LANES), F32)],
        scratch_shapes=[pltpu.VMEM((SUBLANES, LANES), F32)],
        compiler_params=_params(1),
        name="route",
    )(logits)


def _issue_row_gather(idx_ref, idx_base, n_rows, src_hbm, dst_buf, sem):
    def body(g, carry):
        for u in range(GATHER_UNROLL):
            r = g * GATHER_UNROLL + u
            t = idx_ref[idx_base + r]
            pltpu.make_async_copy(
                src_hbm.at[pl.ds(pl.multiple_of(t * ROW_SLAB, ROW_SLAB), ROW_SLAB), :],
                dst_buf.at[pl.ds(pl.multiple_of(r * ROW_SLAB, ROW_SLAB), ROW_SLAB), :],
                sem).start()
        return carry
    lax.fori_loop(0, n_rows // GATHER_UNROLL, body, 0)


def _wait_row_gather(n_rows, src_hbm, dst_buf, sem):
    pltpu.make_async_copy(src_hbm.at[pl.ds(0, n_rows * ROW_SLAB), :], dst_buf, sem).wait()


def _dispatch_kernel(dest_ref, pad_ref, z_ref, xs_hbm, stage, zeros, sem, zsem):
    i = pl.program_id(0)
    n = pl.num_programs(0)
    tm = TM_DISP
    pad_rows = TM_MOE * ROW_SLAB

    def zero_copy(slot, n_slots):
        start = pl.multiple_of(slot * ROW_SLAB, ROW_SLAB)
        return pltpu.make_async_copy(zeros.at[pl.ds(0, n_slots * ROW_SLAB), :],
                                     xs_hbm.at[pl.ds(start, n_slots * ROW_SLAB), :], zsem)

    def for_pad_pieces(e, fn):
        slot = pad_ref[e]
        n = pad_ref[N_EXPERTS + e]
        size = TM_MOE // 2
        while size >= 1:
            piece = zero_copy(slot, size)
            pl.when((n & size) != 0)(functools.partial(fn, piece))
            slot = slot + (n & size)
            size //= 2

    @pl.when(i == 0)
    def _():
        zeros[...] = jnp.zeros(zeros.shape, U32)
        first_free = pad_ref[2 * N_EXPERTS]
        n_total = xs_hbm.shape[0] // pad_rows

        def start_all(e, carry):
            for_pad_pieces(e, lambda piece: piece.start())
            return carry

        def wait_all(e, carry):
            for_pad_pieces(e, lambda piece: piece.wait())
            return carry

        def start_free(b, carry):
            zero_copy(b * TM_MOE, TM_MOE).start()
            return carry

        def wait_free(b, carry):
            zero_copy(b * TM_MOE, TM_MOE).wait()
            return carry

        lax.fori_loop(0, N_EXPERTS, start_all, 0)
        lax.fori_loop(first_free, n_total, start_free, 0)
        lax.fori_loop(0, N_EXPERTS, wait_all, 0)
        lax.fori_loop(first_free, n_total, wait_free, 0)

    cur = i % 2
    stage[cur] = z_ref[...]

    def body(g, carry):
        for u in range(GATHER_UNROLL // TOP_K):
            r = g * (GATHER_UNROLL // TOP_K) + u
            src = stage.at[cur, pl.ds(pl.multiple_of(r * ROW_SLAB, ROW_SLAB), ROW_SLAB), :]
            for k in range(TOP_K):
                d = dest_ref[(i * tm + r) * TOP_K + k]
                pltpu.make_async_copy(
                    src, xs_hbm.at[pl.ds(pl.multiple_of(d * ROW_SLAB, ROW_SLAB), ROW_SLAB), :],
                    sem.at[cur]).start()
        return carry

    lax.fori_loop(0, tm * TOP_K // GATHER_UNROLL, body, 0)

    def drain(slot):
        for _ in range(TOP_K):
            pltpu.make_async_copy(stage.at[slot], xs_hbm.at[pl.ds(0, tm * ROW_SLAB), :], sem.at[slot]).wait()

    @pl.when(i > 0)
    def _():
        drain(1 - cur)

    @pl.when(i == n - 1)
    def _():
        drain(cur)


def _dispatch(dest_flat, pad_start, z, n_slots):
    t = z.shape[0] // ROW_SLAB
    tm = TM_DISP
    return pl.pallas_call(
        _dispatch_kernel,
        grid_spec=pltpu.PrefetchScalarGridSpec(
            num_scalar_prefetch=2,
            grid=(t // tm,),
            in_specs=[pl.BlockSpec((tm * ROW_SLAB, LANES), lambda i, dr, pr: (i, 0))],
            out_specs=pl.BlockSpec(memory_space=pl.ANY),
            scratch_shapes=[pltpu.VMEM((2, tm * ROW_SLAB, LANES), U32),
                            pltpu.VMEM((TM_MOE * ROW_SLAB, LANES), U32),
                            pltpu.SemaphoreType.DMA((2,)), pltpu.SemaphoreType.DMA(())]),
        out_shape=jax.ShapeDtypeStruct((n_slots * ROW_SLAB, LANES), U32),
        compiler_params=_params(1),
        name="dispatch",
    )(dest_flat, pad_start, z)


def _moe_up_kernel(be_ref, nu_ref, xs_ref, wg_ref, bg_ref, wu_ref, bu_ref, hid_ref):
    b = pl.program_id(0)
    tm = TM_MOE

    @pl.when(b < nu_ref[0])
    def _():
        lo, hi = _unpack_rows(_load_row_slabs(xs_ref, 0, tm))
        x = jnp.concatenate([lo.astype(BF16), hi.astype(BF16)], axis=-1)
        g = jnp.dot(x, wg_ref[0], preferred_element_type=F32) + bg_ref[0]
        u = jnp.dot(x, wu_ref[0], preferred_element_type=F32) + bu_ref[0]
        g = jnp.minimum(g, SWIGLU_LIMIT)
        u = jnp.clip(u, -SWIGLU_LIMIT, SWIGLU_LIMIT)
        hid_ref[...] = ((u + 1.0) * (g * jax.nn.sigmoid(SWIGLU_ALPHA * g))).astype(BF16)

    @pl.when(b >= nu_ref[0])
    def _():
        hid_ref[...] = jnp.zeros(hid_ref.shape, BF16)


def _moe_up(block_e, n_used, xs, w_gate, b_gate, w_up, b_up):
    e, d, f = w_gate.shape
    n_blocks = block_e.shape[0]
    tm = TM_MOE
    wspec = pl.BlockSpec((1, d, f), lambda b, be, nu: (be[b], 0, 0))
    bspec = pl.BlockSpec((1, 1, f), lambda b, be, nu: (be[b], 0, 0))
    return pl.pallas_call(
        _moe_up_kernel,
        grid_spec=pltpu.PrefetchScalarGridSpec(
            num_scalar_prefetch=2,
            grid=(n_blocks,),
            in_specs=[pl.BlockSpec((tm * ROW_SLAB, LANES), lambda b, be, nu: (jnp.minimum(b, nu[0] - 1), 0)),
                      wspec, bspec, wspec, bspec],
            out_specs=pl.BlockSpec((tm, f), lambda b, be, nu: (b, 0))),
        out_shape=jax.ShapeDtypeStruct((n_blocks * tm, f), BF16),
        compiler_params=_params(1),
        name="moe_up",
    )(block_e, n_used, xs, w_gate, b_gate, w_up, b_up)


def _moe_down_kernel(be_ref, nu_ref, hid_ref, wd_ref, bd_ref, ys_ref):
    b = pl.program_id(0)

    @pl.when(b < nu_ref[0])
    def _():
        out = jnp.dot(hid_ref[...], wd_ref[0], preferred_element_type=F32) + bd_ref[0]
        _store_row_slabs(ys_ref, _pack_rows(out))

    @pl.when(b >= nu_ref[0])
    def _():
        ys_ref[...] = jnp.zeros(ys_ref.shape, U32)


def _moe_down(block_e, n_used, hid, w_down, b_down):
    e, f, d = w_down.shape
    n_blocks = block_e.shape[0]
    tm = TM_MOE
    return pl.pallas_call(
        _moe_down_kernel,
        grid_spec=pltpu.PrefetchScalarGridSpec(
            num_scalar_prefetch=2,
            grid=(n_blocks,),
            in_specs=[pl.BlockSpec((tm, f), lambda b, be, nu: (b, 0)),
                      pl.BlockSpec((1, f, d), lambda b, be, nu: (be[b], 0, 0)),
                      pl.BlockSpec((1, 1, d), lambda b, be, nu: (be[b], 0, 0))],
            out_specs=pl.BlockSpec((tm * ROW_SLAB, LANES), lambda b, be, nu: (b, 0))),
        out_shape=jax.ShapeDtypeStruct((n_blocks * tm * ROW_SLAB, LANES), U32),
        compiler_params=_params(1),
        name="moe_down",
    )(block_e, n_used, hid, w_down, b_down)


def _combine_kernel(dest_ref, ys_hbm, x1_ref, rw_ref, g2_ref, o_ref, buf, sem):
    i = pl.program_id(0)
    n = pl.num_programs(0)
    tm = TM_COMB
    rows = TOP_K * tm

    @pl.when(i == 0)
    def _():
        _issue_row_gather(dest_ref, 0, rows, ys_hbm, buf.at[0], sem.at[0])

    @pl.when(i + 1 < n)
    def _():
        nxt = (i + 1) % 2
        _issue_row_gather(dest_ref, (i + 1) * rows, rows, ys_hbm, buf.at[nxt], sem.at[nxt])

    cur = i % 2
    _wait_row_gather(rows, ys_hbm, buf.at[cur], sem.at[cur])
    rw = rw_ref[...]
    y_lo = jnp.zeros((tm, HALF), F32)
    y_hi = jnp.zeros((tm, HALF), F32)
    for k in range(TOP_K):
        lo, hi = _unpack_rows(_load_row_slabs(buf.at[cur], k * tm, tm))
        y_lo = y_lo + rw[:, k:k + 1] * lo
        y_hi = y_hi + rw[:, k:k + 1] * hi
    o_ref[...] = x1_ref[...] + g2_ref[0] * jnp.concatenate([y_lo, y_hi], axis=-1)


def _combine(dest_km, ys, x1, rw, g2, seq):
    t, d = x1.shape
    tm = TM_COMB
    per_seq = seq // tm
    return pl.pallas_call(
        _combine_kernel,
        grid_spec=pltpu.PrefetchScalarGridSpec(
            num_scalar_prefetch=1,
            grid=(t // tm,),
            in_specs=[pl.BlockSpec(memory_space=pl.ANY),
                      pl.BlockSpec((tm, d), lambda i, dr: (i, 0)),
                      pl.BlockSpec((tm, LANES), lambda i, dr: (i, 0)),
                      pl.BlockSpec((1, 1, d), lambda i, dr: (i // per_seq, 0, 0))],
            out_specs=pl.BlockSpec((tm, d), lambda i, dr: (i, 0)),
            scratch_shapes=[pltpu.VMEM((2, TOP_K * tm * ROW_SLAB, LANES), U32),
                            pltpu.SemaphoreType.DMA((2,))]),
        out_shape=jax.ShapeDtypeStruct((t, d), F32),
        compiler_params=_params(1),
        name="combine",
    )(dest_km, ys, x1, rw, g2)


def _pad_cols(w, n):
    return jnp.pad(w, ((0, 0), (0, n - w.shape[1])))


def _layer(x, mod, norm_mix_g, norm_ffn_g, w_in, b_f, q_norm_g, k_norm_g, conv_w, conv_b,
           conv_ln_g, conv_ln_b, w_out, w_router, b_router, w_gate, b_gate, w_up, b_up,
           w_down, b_down):
    b, s, d = x.shape
    t = b * s
    shift1, scale1, gate1, shift2, scale2, gate2 = [m[:, None, :] for m in jnp.split(mod, 6, axis=-1)]
    a1 = norm_mix_g[None, None, :] * (1.0 + scale1)
    a2 = norm_ffn_g[None, None, :] * (1.0 + scale2)

    aw = ATTN_WIDTH
    w_cat = jnp.concatenate([w_in[:, :3 * aw], w_in[:, 3 * aw + ATTN_HEADS:]], axis=1).astype(BF16)
    w_f = _pad_cols(w_in[:, 3 * aw:3 * aw + ATTN_HEADS], LANES).astype(BF16)
    b_f_pad = _pad_cols(b_f[None, :], LANES)

    q, k, v, u, logf = _in_proj(x, a1, shift1, w_cat, w_f, b_f_pad,
                                q_norm_g[None, :], k_norm_g[None, :])
    attn = _fox_attention(q, k, v, _forget_cumsum(logf))
    conv = _conv_module(u, conv_w, conv_b[None, :], conv_ln_g[None, :], conv_ln_b[None, :])

    wr = _pad_cols(w_router, LANES)
    wr_hi = wr.astype(BF16)
    wr_lo = (wr - wr_hi.astype(F32)).astype(BF16)
    b_r = jnp.concatenate([b_router, jnp.full((LANES - N_EXPERTS,), NEG_BIG, F32)])[None, :]
    x1, z, logits = _out_proj(attn, conv, x, w_out.astype(BF16), gate1, a2, shift2, wr_hi, wr_lo, b_r)

    ri, rw, cnt = _route(logits.reshape(t, LANES))

    tm = TM_MOE
    n_blocks = t * TOP_K // tm + N_EXPERTS
    idx = ri[:, 0:TOP_K]
    rank = ri[:, TOP_K:2 * TOP_K]
    counts = cnt[0, :N_EXPERTS].astype(I32)
    padded = (counts + tm - 1) // tm * tm
    padded_end = jnp.cumsum(padded)
    padded_start = padded_end - padded
    dest = padded_start[idx] + rank
    block_start = jnp.arange(n_blocks, dtype=I32)[:, None] * tm
    block_e = jnp.minimum(jnp.sum((padded_end[None, :] <= block_start).astype(I32), axis=1), N_EXPERTS - 1)
    n_used = (padded_end[-1:] // tm).astype(I32)
    dest_km = dest.reshape(t // TM_COMB, TM_COMB, TOP_K).transpose(0, 2, 1).reshape(-1)

    pad_table = jnp.concatenate([padded_start + counts, padded - counts, n_used])
    xs = _dispatch(dest.reshape(-1), pad_table, z, n_blocks * tm)
    hid = _moe_up(block_e, n_used, xs, w_gate.astype(BF16), b_gate[:, None, :],
                  w_up.astype(BF16), b_up[:, None, :])
    ys = _moe_down(block_e, n_used, hid, w_down.astype(BF16), b_down[:, None, :])
    out = _combine(dest_km, ys, x1.reshape(t, d), rw, gate2, s)
    return out.reshape(b, s, d)


def kernel(x, c, ada_w, ada_b, norm_mix_g, norm_ffn_g, w_in, b_f, q_norm_g, k_norm_g, conv_w, conv_b,
           conv_ln_g, conv_ln_b, w_out, w_router, b_router, w_gate, b_gate, w_up, b_up, w_down, b_down):
    b = x.shape[0]
    c_pad = jnp.pad(c, ((0, SUBLANES - b), (0, 0)))
    for l in range(ada_w.shape[0]):
        mod = _ada_mod(c_pad, ada_w[l], ada_b[l])[:b]
        x = _layer(x, mod, norm_mix_g[l], norm_ffn_g[l], w_in[l], b_f[l], q_norm_g[l], k_norm_g[l],
                   conv_w[l], conv_b[l], conv_ln_g[l], conv_ln_b[l], w_out[l], w_router[l],
                   b_router[l], w_gate[l], b_gate[l], w_up[l], b_up[l], w_down[l], b_down[l])
    return x
```

```python
import functools

import jax
import jax.numpy as jnp
from jax import lax
from jax.experimental import pallas as pl
from jax.experimental.pallas import tpu as pltpu

F32 = jnp.float32
BF16 = jnp.bfloat16
I32 = jnp.int32
U32 = jnp.uint32

D_MODEL = 2048
ATTN_HEADS = 8
HEAD_DIM = 128
ATTN_WIDTH = ATTN_HEADS * HEAD_DIM
CONV_WIDTH = D_MODEL - ATTN_WIDTH
CONV_KERNEL = 31
N_EXPERTS = 32
TOP_K = 4
SWIGLU_LIMIT = 7.0
SWIGLU_ALPHA = 1.702
EPS = 1e-6

LANES = 128
SUBLANES = 8
HALF = D_MODEL // 2
ROW_SLAB = HALF // LANES
VMEM_LIMIT = 56 * 1024 * 1024

TM_PROJ = 512
TQ = 512
CUM_CHUNK = 256
TS_CONV = 256
CONV_HALO = 32
CONV_CHUNK = 32
TM_ROUTE = 512
TM_MOE = 256
TM_DISP = 256
TM_COMB = 256
GATHER_UNROLL = 32
W_CHUNK = 256
N_WCHUNK = D_MODEL // W_CHUNK
N_STAGE = 2
NEG_BIG = -1e30


def _params(n_axes):
    return pltpu.CompilerParams(
        dimension_semantics=("arbitrary",) * n_axes, vmem_limit_bytes=VMEM_LIMIT)


def _resident(shape):
    nd = len(shape)
    return pl.BlockSpec(shape, lambda *_: (0,) * nd, pipeline_mode=pl.Buffered(1))


def _ada_kernel(c_ref, w_ref, b_ref, o_ref):
    c = c_ref[...]
    c_act = (c * jax.nn.sigmoid(c)).astype(BF16)
    o_ref[...] = jnp.dot(c_act, w_ref[...].astype(BF16), preferred_element_type=F32) + b_ref[...]


def _ada_mod(c_pad, ada_w, ada_b):
    rows, d = c_pad.shape
    n = ada_w.shape[1]
    tn = 1024
    return pl.pallas_call(
        _ada_kernel,
        grid=(n // tn,),
        in_specs=[pl.BlockSpec((rows, d), lambda j: (0, 0)),
                  pl.BlockSpec((d, tn), lambda j: (0, j)),
                  pl.BlockSpec((1, tn), lambda j: (0, j))],
        out_specs=pl.BlockSpec((rows, tn), lambda j: (0, j)),
        out_shape=jax.ShapeDtypeStruct((rows, n), F32),
        compiler_params=_params(1),
        name="ada_mod",
    )(c_pad, ada_w, ada_b.reshape(1, n))


def _log_sigmoid(x):
    return jnp.minimum(x, 0.0) - jnp.log1p(jnp.exp(-jnp.abs(x)))


def _head_rms(y, g):
    outs = []
    for h in range(ATTN_HEADS):
        yh = y[:, h * HEAD_DIM:(h + 1) * HEAD_DIM]
        r = lax.rsqrt(jnp.mean(yh * yh, axis=-1, keepdims=True) + EPS)
        outs.append(yh * r * g)
    return jnp.concatenate(outs, axis=-1)


def _inproj_kernel(x_ref, a_ref, s_ref, w_ref, wf_ref, bf_ref, qg_ref, kg_ref,
                   q_ref, k_ref, v_ref, u_ref, f_ref):
    x = x_ref[0]
    h = x * lax.rsqrt(jnp.mean(x * x, axis=-1, keepdims=True) + EPS) * a_ref[0] + s_ref[0]
    hb = h.astype(BF16)
    aw = ATTN_WIDTH
    q = jnp.dot(hb, w_ref[:, 0:aw], preferred_element_type=F32)
    q_ref[0] = (_head_rms(q, qg_ref[...]) * (HEAD_DIM ** -0.5)).astype(BF16)
    k = jnp.dot(hb, w_ref[:, aw:2 * aw], preferred_element_type=F32)
    k_ref[0] = _head_rms(k, kg_ref[...]).astype(BF16)
    v_ref[0] = jnp.dot(hb, w_ref[:, 2 * aw:3 * aw], preferred_element_type=F32).astype(BF16)
    a = jnp.dot(hb, w_ref[:, 3 * aw:3 * aw + CONV_WIDTH], preferred_element_type=F32)
    g = jnp.dot(hb, w_ref[:, 3 * aw + CONV_WIDTH:], preferred_element_type=F32)
    u_ref[0] = (a * jax.nn.sigmoid(g)).astype(BF16)
    fl = jnp.dot(hb, wf_ref[...], preferred_element_type=F32) + bf_ref[...]
    f_ref[0] = _log_sigmoid(fl)


def _in_proj(x, a1, s1, w_cat, w_f, b_f, q_g, k_g):
    b, s, d = x.shape
    tm = TM_PROJ
    ncat = w_cat.shape[1]
    tok = lambda w: pl.BlockSpec((1, tm, w), lambda bi, i: (bi, i, 0))
    per_batch = pl.BlockSpec((1, 1, d), lambda bi, i: (bi, 0, 0))
    return pl.pallas_call(
        _inproj_kernel,
        grid=(b, s // tm),
        in_specs=[tok(d), per_batch, per_batch,
                  _resident((d, ncat)), _resident((d, LANES)), _resident((1, LANES)),
                  _resident((1, HEAD_DIM)), _resident((1, HEAD_DIM))],
        out_specs=[tok(ATTN_WIDTH), tok(ATTN_WIDTH), tok(ATTN_WIDTH), tok(CONV_WIDTH), tok(LANES)],
        out_shape=[jax.ShapeDtypeStruct((b, s, ATTN_WIDTH), BF16)] * 3
        + [jax.ShapeDtypeStruct((b, s, CONV_WIDTH), BF16),
           jax.ShapeDtypeStruct((b, s, LANES), F32)],
        compiler_params=_params(2),
        name="in_proj",
    )(x, a1, s1, w_cat, w_f, b_f, q_g, k_g)


def _bf16_pieces(c):
    p0 = c.astype(BF16)
    r0 = c - p0.astype(F32)
    p1 = r0.astype(BF16)
    p2 = (r0 - p1.astype(F32)).astype(BF16)
    return p0, p1, p2


def _cumsum_kernel(f_ref, c_ref):
    ch = CUM_CHUNK
    s = f_ref.shape[1]
    row = lax.broadcasted_iota(I32, (ch, ch), 0)
    col = lax.broadcasted_iota(I32, (ch, ch), 1)
    tri = (col <= row).astype(BF16)
    carry = jnp.zeros((1, LANES), F32)
    for i in range(s // ch):
        cs = carry
        for p in _bf16_pieces(f_ref[0, i * ch:(i + 1) * ch, :]):
            cs = cs + jnp.dot(tri, p, preferred_element_type=F32)
        c_ref[0, i * ch:(i + 1) * ch, :] = cs
        carry = cs[ch - 1:ch, :]


def _forget_cumsum(logf):
    b, s, _ = logf.shape
    return pl.pallas_call(
        _cumsum_kernel,
        grid=(b,),
        in_specs=[pl.BlockSpec((1, s, LANES), lambda bi: (bi, 0, 0))],
        out_specs=pl.BlockSpec((1, s, LANES), lambda bi: (bi, 0, 0)),
        out_shape=jax.ShapeDtypeStruct((b, s, LANES), F32),
        compiler_params=_params(1),
        name="forget_cumsum",
    )(logf)


def _attn_kernel(q_ref, k_ref, v_ref, cq_ref, ck_ref, o_ref, kx_sc, s_sc, m_sc, l_sc, acc_sc):
    h = pl.program_id(1)
    qi = pl.program_id(2)
    tq = TQ
    n_kv = k_ref.shape[1] // tq
    sel_r = lax.broadcasted_iota(I32, (LANES, LANES), 0)
    sel_c = lax.broadcasted_iota(I32, (LANES, LANES), 1)
    lane1 = lax.broadcasted_iota(I32, (1, LANES), 1)

    def spread(c, first_lane, sign):
        out = jnp.zeros(c.shape, F32)
        for i, p in enumerate(_bf16_pieces(c)):
            sel = jnp.where((sel_r == h) & (sel_c == first_lane + i), sign, 0.0).astype(BF16)
            out = out + jnp.dot(p, sel, preferred_element_type=F32)
        return out

    @pl.when(qi == 0)
    def _():
        ones_k = jnp.where((lane1 >= 3) & (lane1 < 6), 1.0, 0.0)

        def fill(i, carry):
            st = pl.multiple_of(i * tq, tq)
            kx_sc[pl.ds(st, tq), :] = (spread(ck_ref[0, pl.ds(st, tq), :], 0, -1.0) + ones_k).astype(BF16)
            return carry

        lax.fori_loop(0, n_kv, fill, 0)

    ones_q = jnp.where(lane1 < 3, 1.0, 0.0)
    qx = (spread(cq_ref[0], 3, 1.0) + ones_q).astype(BF16)
    q_aug = jnp.concatenate([q_ref[0], qx], axis=-1)

    m_sc[...] = jnp.full(m_sc.shape, -jnp.inf, F32)
    l_sc[...] = jnp.zeros(l_sc.shape, F32)
    acc_sc[...] = jnp.zeros(acc_sc.shape, F32)

    def scores_into(j, slot):
        st = pl.multiple_of(j * tq, tq)
        k_aug = jnp.concatenate([k_ref[0, pl.ds(st, tq), :], kx_sc[pl.ds(st, tq), :]], axis=-1)
        s_sc[slot] = lax.dot_general(k_aug, q_aug, (((1,), (1,)), ((), ())),
                                     preferred_element_type=F32)

    def update(j, slot, masked):
        st = pl.multiple_of(j * tq, tq)
        s = s_sc[slot]
        if masked:
            key = lax.broadcasted_iota(I32, (tq, tq), 0)
            qry = lax.broadcasted_iota(I32, (tq, tq), 1)
            s = jnp.where(key <= qry, s, -jnp.inf)
        m_prev = m_sc[...]
        m_new = jnp.maximum(m_prev, jnp.max(s, axis=0, keepdims=True))
        alpha = jnp.exp(m_prev - m_new)
        p = jnp.exp(s - m_new)
        l_sc[...] = alpha * l_sc[...] + jnp.sum(p, axis=0, keepdims=True)
        pv = lax.dot_general(v_ref[0, pl.ds(st, tq), :], p.astype(BF16), (((0,), (0,)), ((), ())),
                             preferred_element_type=F32)
        acc_sc[...] = alpha * acc_sc[...] + pv
        m_sc[...] = m_new

    scores_into(0, 0)

    def body(jj, carry):
        j = 2 * jj
        scores_into(j + 1, 1)
        update(j, 0, False)
        scores_into(j + 2, 0)
        update(j + 1, 1, False)
        return carry

    lax.fori_loop(0, qi // 2, body, 0)

    @pl.when(qi % 2 == 1)
    def _():
        scores_into(qi, 1)
        update(qi - 1, 0, False)
        update(qi, 1, True)

    @pl.when(qi % 2 == 0)
    def _():
        update(qi, 0, True)

    o_ref[0] = (acc_sc[...] / l_sc[...]).T.astype(BF16)


def _fox_attention(q, k, v, cum):
    b, s, _ = q.shape
    tq = TQ
    return pl.pallas_call(
        _attn_kernel,
        grid=(b, ATTN_HEADS, s // tq),
        in_specs=[pl.BlockSpec((1, tq, HEAD_DIM), lambda bi, h, i: (bi, i, h)),
                  pl.BlockSpec((1, s, HEAD_DIM), lambda bi, h, i: (bi, 0, h)),
                  pl.BlockSpec((1, s, HEAD_DIM), lambda bi, h, i: (bi, 0, h)),
                  pl.BlockSpec((1, tq, LANES), lambda bi, h, i: (bi, i, 0)),
                  pl.BlockSpec((1, s, LANES), lambda bi, h, i: (bi, 0, 0))],
        out_specs=pl.BlockSpec((1, tq, HEAD_DIM), lambda bi, h, i: (bi, i, h)),
        out_shape=jax.ShapeDtypeStruct((b, s, ATTN_WIDTH), BF16),
        scratch_shapes=[pltpu.VMEM((s, LANES), BF16),
                        pltpu.VMEM((2, tq, tq), F32),
                        pltpu.VMEM((1, tq), F32), pltpu.VMEM((1, tq), F32),
                        pltpu.VMEM((HEAD_DIM, tq), F32)],
        compiler_params=_params(3),
        name="fox_attention",
    )(q, k, v, cum, cum)


def _conv_kernel(u_ref, w_ref, cb_ref, lg_ref, lb_ref, o_ref, ubuf, shifted, wb):
    i = pl.program_id(1)
    ts, halo, ck = TS_CONV, CONV_HALO, CONV_CHUNK

    @pl.when(i == 0)
    def _():
        ubuf[0:halo, :] = jnp.zeros((halo, CONV_WIDTH), F32)
        for j in range(CONV_KERNEL):
            wb[j * SUBLANES:(j + 1) * SUBLANES, :] = jnp.broadcast_to(
                w_ref[j:j + 1, :], (SUBLANES, CONV_WIDTH))

    ubuf[halo:halo + ts, :] = u_ref[0].astype(F32)
    base = halo - (CONV_KERNEL - 1)
    span = shifted.shape[1]
    for r in range(1, SUBLANES):
        shifted[r] = ubuf[r:r + span, :]
    for c in range(ts // ck):
        acc = jnp.zeros((ck, CONV_WIDTH), F32)
        for j in range(CONV_KERNEL):
            off = c * ck + base + j
            r, al = off % SUBLANES, off - off % SUBLANES
            slab = ubuf[al:al + ck, :] if r == 0 else shifted[r, al:al + ck, :]
            wj = wb[j * SUBLANES:(j + 1) * SUBLANES, :]
            acc = acc + slab * jnp.concatenate([wj] * (ck // SUBLANES), axis=0)
        y = acc + cb_ref[...]
        mu = jnp.mean(y, axis=-1, keepdims=True)
        yc = y - mu
        var = jnp.mean(yc * yc, axis=-1, keepdims=True)
        z = yc * lax.rsqrt(var + EPS) * lg_ref[...] + lb_ref[...]
        o_ref[0, c * ck:(c + 1) * ck, :] = (z * jax.nn.sigmoid(z)).astype(BF16)
    ubuf[0:halo, :] = ubuf[ts:ts + halo, :]


def _conv_module(u, conv_w, conv_b, ln_g, ln_b):
    b, s, cw = u.shape
    ts = TS_CONV
    return pl.pallas_call(
        _conv_kernel,
        grid=(b, s // ts),
        in_specs=[pl.BlockSpec((1, ts, cw), lambda bi, i: (bi, i, 0)),
                  _resident((CONV_KERNEL, cw)), _resident((1, cw)),
                  _resident((1, cw)), _resident((1, cw))],
        out_specs=pl.BlockSpec((1, ts, cw), lambda bi, i: (bi, i, 0)),
        out_shape=jax.ShapeDtypeStruct((b, s, cw), BF16),
        scratch_shapes=[pltpu.VMEM((CONV_HALO + ts, cw), F32),
                        pltpu.VMEM((SUBLANES, ts + CONV_HALO - SUBLANES, cw), F32),
                        pltpu.VMEM((CONV_KERNEL * SUBLANES, cw), F32)],
        compiler_params=_params(2),
        name="conv_module",
    )(u, conv_w, conv_b, ln_g, ln_b)


def _pack_rows(val):
    lo = lax.bitcast_convert_type(val[:, :HALF].astype(BF16).astype(F32), U32)
    hi = lax.bitcast_convert_type(val[:, HALF:].astype(BF16).astype(F32), U32)
    return hi | (lo >> 16)


def _unpack_rows(word):
    lo = lax.bitcast_convert_type(word << 16, F32)
    hi = lax.bitcast_convert_type(word & jnp.uint32(0xFFFF0000), F32)
    return lo, hi


def _store_row_slabs(ref, word):
    rows = word.shape[0]
    for s in range(ROW_SLAB):
        ref[pl.ds(s, rows, stride=ROW_SLAB), :] = word[:, s * LANES:(s + 1) * LANES]


def _load_row_slabs(ref, start, rows):
    parts = [ref[pl.ds(start * ROW_SLAB + s, rows, stride=ROW_SLAB), :] for s in range(ROW_SLAB)]
    return jnp.concatenate(parts, axis=-1)


def _outproj_kernel(at_ref, cv_ref, x_ref, wo_ref, g1_ref, a2_ref, s2_ref, wrh_ref, wrl_ref, br_ref,
                    x1_ref, z_ref, lg_ref):
    mix = (jnp.dot(at_ref[0], wo_ref[0:ATTN_WIDTH, :], preferred_element_type=F32)
           + jnp.dot(cv_ref[0], wo_ref[ATTN_WIDTH:, :], preferred_element_type=F32))
    x1 = x_ref[0] + g1_ref[0] * mix
    x1_ref[0] = x1
    h2 = x1 * lax.rsqrt(jnp.mean(x1 * x1, axis=-1, keepdims=True) + EPS) * a2_ref[0] + s2_ref[0]
    _store_row_slabs(z_ref, _pack_rows(h2))
    hi = h2.astype(BF16)
    lo = (h2 - hi.astype(F32)).astype(BF16)
    lg_ref[0] = (jnp.dot(hi, wrh_ref[...], preferred_element_type=F32)
                 + jnp.dot(lo, wrh_ref[...], preferred_element_type=F32)
                 + jnp.dot(hi, wrl_ref[...], preferred_element_type=F32)) + br_ref[...]


def _out_proj(attn, conv, x, w_out, g1, a2, s2, wr_hi, wr_lo, b_r):
    b, s, d = x.shape
    tm = TM_PROJ
    nt = s // tm
    tok = lambda w: pl.BlockSpec((1, tm, w), lambda bi, i: (bi, i, 0))
    per_batch = pl.BlockSpec((1, 1, d), lambda bi, i: (bi, 0, 0))
    return pl.pallas_call(
        _outproj_kernel,
        grid=(b, nt),
        in_specs=[tok(ATTN_WIDTH), tok(CONV_WIDTH), tok(d), _resident((d, d)),
                  per_batch, per_batch, per_batch,
                  _resident((d, LANES)), _resident((d, LANES)), _resident((1, LANES))],
        out_specs=[tok(d),
                   pl.BlockSpec((tm * ROW_SLAB, LANES), lambda bi, i: (bi * nt + i, 0)),
                   tok(LANES)],
        out_shape=[jax.ShapeDtypeStruct((b, s, d), F32),
                   jax.ShapeDtypeStruct((b * s * ROW_SLAB, LANES), U32),
                   jax.ShapeDtypeStruct((b, s, LANES), F32)],
        compiler_params=_params(2),
        name="out_proj",
    )(attn, conv, x, w_out, g1, a2, s2, wr_hi, wr_lo, b_r)


def _route_kernel(lg_ref, ri_ref, rw_ref, cnt_ref, carry):
    i = pl.program_id(0)
    tm = TM_ROUTE

    @pl.when(i == 0)
    def _():
        carry[...] = jnp.zeros(carry.shape, F32)

    l = lg_ref[...]
    lane = lax.broadcasted_iota(I32, (tm, LANES), 1).astype(F32)
    vals, idxs = [], []
    for _ in range(TOP_K):
        m = jnp.max(l, axis=-1, keepdims=True)
        ix = jnp.min(jnp.where(l == m, lane, float(LANES)), axis=-1, keepdims=True)
        vals.append(m)
        idxs.append(ix)
        l = jnp.where(lane == ix, -jnp.inf, l)
    es = [jnp.exp(v - vals[0]) for v in vals]
    den = es[0] + es[1] + es[2] + es[3]
    onehot = jnp.zeros((tm, LANES), F32)
    for ix in idxs:
        onehot = onehot + jnp.where(lane == ix, 1.0, 0.0)
    row = lax.broadcasted_iota(I32, (tm, tm), 0)
    col = lax.broadcasted_iota(I32, (tm, tm), 1)
    strict = (col < row).astype(BF16)
    before = jnp.dot(strict, onehot.astype(BF16), preferred_element_type=F32) + carry[0:1, :]
    ri = jnp.zeros((tm, LANES), F32)
    rw = jnp.zeros((tm, LANES), F32)
    for k in range(TOP_K):
        rank = jnp.sum(jnp.where(lane == idxs[k], before, 0.0), axis=-1, keepdims=True)
        ri = jnp.where(lane == k, idxs[k], ri)
        ri = jnp.where(lane == TOP_K + k, rank, ri)
        rw = jnp.where(lane == k, es[k] / den, rw)
    ri_ref[...] = ri.astype(I32)
    rw_ref[...] = rw
    carry[0:1, :] = carry[0:1, :] + jnp.sum(onehot, axis=0, keepdims=True)
    cnt_ref[...] = carry[...]


def _route(logits):
    t = logits.shape[0]
    tm = TM_ROUTE
    return pl.pallas_call(
        _route_kernel,
        grid=(t // tm,),
        in_specs=[pl.BlockSpec((tm, LANES), lambda i: (i, 0))],
        out_specs=[pl.BlockSpec((tm, LANES), lambda i: (i, 0)),
                   pl.BlockSpec((tm, LANES), lambda i: (i, 0)),
                   pl.BlockSpec((SUBLANES, LANES), lambda i: (0, 0))],
        out_shape=[jax.ShapeDtypeStruct((t, LANES), I32),
                   jax.ShapeDtypeStruct((t, LANES), F32),
                   jax.ShapeDtypeStruct((SUBLANES, LANES), F32)],
        scratch_shapes=[pltpu.VMEM((SUBLANES, LANES), F32)],
        compiler_params=_params(1),
        name="route",
    )(logits)


def _issue_row_gather(idx_ref, idx_base, n_rows, src_hbm, dst_buf, sem):
    def body(g, carry):
        for u in range(GATHER_UNROLL):
            r = g * GATHER_UNROLL + u
            t = idx_ref[idx_base + r]
            pltpu.make_async_copy(
                src_hbm.at[pl.ds(pl.multiple_of(t * ROW_SLAB, ROW_SLAB), ROW_SLAB), :],
                dst_buf.at[pl.ds(pl.multiple_of(r * ROW_SLAB, ROW_SLAB), ROW_SLAB), :],
                sem).start(priority=u % 2)
        return carry
    lax.fori_loop(0, n_rows // GATHER_UNROLL, body, 0)


def _wait_row_gather(n_rows, src_hbm, dst_buf, sem):
    pltpu.make_async_copy(src_hbm.at[pl.ds(0, n_rows * ROW_SLAB), :], dst_buf, sem).wait()


def _dispatch_kernel(dest_ref, pad_ref, z_ref, xs_hbm, stage, zeros, sem, zsem):
    i = pl.program_id(0)
    n = pl.num_programs(0)
    tm = TM_DISP
    pad_rows = TM_MOE * ROW_SLAB

    def zero_copy(slot, n_slots):
        start = pl.multiple_of(slot * ROW_SLAB, ROW_SLAB)
        return pltpu.make_async_copy(zeros.at[pl.ds(0, n_slots * ROW_SLAB), :],
                                     xs_hbm.at[pl.ds(start, n_slots * ROW_SLAB), :], zsem)

    def for_pad_pieces(e, fn):
        slot = pad_ref[e]
        n = pad_ref[N_EXPERTS + e]
        size = TM_MOE // 2
        while size >= 1:
            piece = zero_copy(slot, size)
            pl.when((n & size) != 0)(functools.partial(fn, piece))
            slot = slot + (n & size)
            size //= 2

    @pl.when(i == 0)
    def _():
        zeros[...] = jnp.zeros(zeros.shape, U32)
        first_free = pad_ref[2 * N_EXPERTS]
        n_total = xs_hbm.shape[0] // pad_rows

        def start_all(e, carry):
            for_pad_pieces(e, lambda piece: piece.start())
            return carry

        def wait_all(e, carry):
            for_pad_pieces(e, lambda piece: piece.wait())
            return carry

        def start_free(b, carry):
            zero_copy(b * TM_MOE, TM_MOE).start()
            return carry

        def wait_free(b, carry):
            zero_copy(b * TM_MOE, TM_MOE).wait()
            return carry

        lax.fori_loop(0, N_EXPERTS, start_all, 0)
        lax.fori_loop(first_free, n_total, start_free, 0)
        lax.fori_loop(0, N_EXPERTS, wait_all, 0)
        lax.fori_loop(first_free, n_total, wait_free, 0)

    cur = i % 2
    stage[cur] = z_ref[...]

    def body(g, carry):
        for u in range(GATHER_UNROLL // TOP_K):
            r = g * (GATHER_UNROLL // TOP_K) + u
            src = stage.at[cur, pl.ds(pl.multiple_of(r * ROW_SLAB, ROW_SLAB), ROW_SLAB), :]
            for k in range(TOP_K):
                d = dest_ref[(i * tm + r) * TOP_K + k]
                pltpu.make_async_copy(
                    src, xs_hbm.at[pl.ds(pl.multiple_of(d * ROW_SLAB, ROW_SLAB), ROW_SLAB), :],
                    sem.at[cur]).start(priority=k % 2)
        return carry

    lax.fori_loop(0, tm * TOP_K // GATHER_UNROLL, body, 0)

    def drain(slot):
        for _ in range(TOP_K):
            pltpu.make_async_copy(stage.at[slot], xs_hbm.at[pl.ds(0, tm * ROW_SLAB), :], sem.at[slot]).wait()

    @pl.when(i > 0)
    def _():
        drain(1 - cur)

    @pl.when(i == n - 1)
    def _():
        drain(cur)


def _dispatch(dest_flat, pad_start, z, n_slots):
    t = z.shape[0] // ROW_SLAB
    tm = TM_DISP
    return pl.pallas_call(
        _dispatch_kernel,
        grid_spec=pltpu.PrefetchScalarGridSpec(
            num_scalar_prefetch=2,
            grid=(t // tm,),
            in_specs=[pl.BlockSpec((tm * ROW_SLAB, LANES), lambda i, dr, pr: (i, 0))],
            out_specs=pl.BlockSpec(memory_space=pl.ANY),
            scratch_shapes=[pltpu.VMEM((2, tm * ROW_SLAB, LANES), U32),
                            pltpu.VMEM((TM_MOE * ROW_SLAB, LANES), U32),
                            pltpu.SemaphoreType.DMA((2,)), pltpu.SemaphoreType.DMA(())]),
        out_shape=jax.ShapeDtypeStruct((n_slots * ROW_SLAB, LANES), U32),
        compiler_params=_params(1),
        name="dispatch",
    )(dest_flat, pad_start, z)


SCHED_W = 5
S_EXPERT, S_SLOT, S_NEXT, S_C0, S_C1 = range(SCHED_W)


def _expert_weights_step(sched_ref, b, n_used, w_hbm, wbuf, stage, sems, compute):
    n_mat = len(w_hbm)
    n_chunk = wbuf.shape[2] // W_CHUNK

    def rows(rc):
        return pl.ds(pl.multiple_of(rc * W_CHUNK, W_CHUNK), W_CHUNK)

    def copies(e, rc, j):
        return [pltpu.make_async_copy(w_hbm[m].at[e, rows(rc), :], stage.at[j, m], sems.at[j, m])
                for m in range(n_mat)]

    def convert(slot, rc, j):
        for m in range(n_mat):
            wbuf[slot, m, rows(rc), :] = stage[j, m].astype(BF16)

    def load_now(e, slot, rc0, rc1):
        def body(rc, carry):
            for cp in copies(e, rc, 0):
                cp.start()
            for cp in copies(e, rc, 0):
                cp.wait()
            convert(slot, rc, 0)
            return carry
        lax.fori_loop(rc0, rc1, body, 0)

    @pl.when(b == 0)
    def _():
        load_now(sched_ref[S_EXPERT], 0, 0, n_chunk)

    @pl.when(b < n_used)
    def _():
        base = b * SCHED_W
        slot, nxt = sched_ref[base + S_SLOT], sched_ref[base + S_NEXT]
        c0, c1 = sched_ref[base + S_C0], sched_ref[base + S_C1]
        for j in range(N_STAGE):
            @pl.when(c0 + j < c1)
            def _():
                for cp in copies(nxt, c0 + j, j):
                    cp.start()
        compute(slot)
        for j in range(N_STAGE):
            @pl.when(c0 + j < c1)
            def _():
                for cp in copies(nxt, c0 + j, j):
                    cp.wait()
                convert(1 - slot, c0 + j, j)
        load_now(nxt, 1 - slot, jnp.minimum(c0 + N_STAGE, c1), c1)


def _moe_up_kernel(sched_ref, nu_ref, xs_ref, wg_hbm, bg_ref, wu_hbm, bu_ref, hid_ref, wbuf, stage, sems):
    b = pl.program_id(0)
    tm = TM_MOE

    def compute(slot):
        lo, hi = _unpack_rows(_load_row_slabs(xs_ref, 0, tm))
        x = jnp.concatenate([lo.astype(BF16), hi.astype(BF16)], axis=-1)
        g = jnp.dot(x, wbuf[slot, 0], preferred_element_type=F32) + bg_ref[0]
        u = jnp.dot(x, wbuf[slot, 1], preferred_element_type=F32) + bu_ref[0]
        g = jnp.minimum(g, SWIGLU_LIMIT)
        u = jnp.clip(u, -SWIGLU_LIMIT, SWIGLU_LIMIT)
        hid_ref[...] = ((u + 1.0) * (g * jax.nn.sigmoid(SWIGLU_ALPHA * g))).astype(BF16)

    _expert_weights_step(sched_ref, b, nu_ref[0], (wg_hbm, wu_hbm), wbuf, stage, sems, compute)

    @pl.when(b >= nu_ref[0])
    def _():
        hid_ref[...] = jnp.zeros(hid_ref.shape, BF16)


def _expert_scratch(n_mat, k, n):
    return [pltpu.VMEM((2, n_mat, k, n), BF16),
            pltpu.VMEM((N_STAGE, n_mat, W_CHUNK, n), F32),
            pltpu.SemaphoreType.DMA((N_STAGE, n_mat))]


def _moe_up(sched, n_used, xs, w_gate, b_gate, w_up, b_up):
    e, d, f = w_gate.shape
    n_blocks = sched.shape[0] // SCHED_W
    tm = TM_MOE
    hbm = pl.BlockSpec(memory_space=pl.ANY)
    bspec = pl.BlockSpec((1, 1, f), lambda b, sc, nu: (sc[b * SCHED_W + S_EXPERT], 0, 0))
    return pl.pallas_call(
        _moe_up_kernel,
        grid_spec=pltpu.PrefetchScalarGridSpec(
            num_scalar_prefetch=2,
            grid=(n_blocks,),
            in_specs=[pl.BlockSpec((tm * ROW_SLAB, LANES), lambda b, sc, nu: (jnp.minimum(b, nu[0] - 1), 0)),
                      hbm, bspec, hbm, bspec],
            out_specs=pl.BlockSpec((tm, f), lambda b, sc, nu: (b, 0)),
            scratch_shapes=_expert_scratch(2, d, f)),
        out_shape=jax.ShapeDtypeStruct((n_blocks * tm, f), BF16),
        compiler_params=_params(1),
        name="moe_up",
    )(sched, n_used, xs, w_gate, b_gate, w_up, b_up)


def _moe_down_kernel(sched_ref, nu_ref, hid_ref, wd_hbm, bd_ref, ys_ref, wbuf, stage, sems):
    b = pl.program_id(0)

    def compute(slot):
        out = jnp.dot(hid_ref[...], wbuf[slot, 0], preferred_element_type=F32) + bd_ref[0]
        _store_row_slabs(ys_ref, _pack_rows(out))

    _expert_weights_step(sched_ref, b, nu_ref[0], (wd_hbm,), wbuf, stage, sems, compute)

    @pl.when(b >= nu_ref[0])
    def _():
        ys_ref[...] = jnp.zeros(ys_ref.shape, U32)


def _moe_down(sched, n_used, hid, w_down, b_down):
    e, f, d = w_down.shape
    n_blocks = sched.shape[0] // SCHED_W
    tm = TM_MOE
    return pl.pallas_call(
        _moe_down_kernel,
        grid_spec=pltpu.PrefetchScalarGridSpec(
            num_scalar_prefetch=2,
            grid=(n_blocks,),
            in_specs=[pl.BlockSpec((tm, f), lambda b, sc, nu: (b, 0)),
                      pl.BlockSpec(memory_space=pl.ANY),
                      pl.BlockSpec((1, 1, d), lambda b, sc, nu: (sc[b * SCHED_W + S_EXPERT], 0, 0))],
            out_specs=pl.BlockSpec((tm * ROW_SLAB, LANES), lambda b, sc, nu: (b, 0)),
            scratch_shapes=_expert_scratch(1, f, d)),
        out_shape=jax.ShapeDtypeStruct((n_blocks * tm * ROW_SLAB, LANES), U32),
        compiler_params=_params(1),
        name="moe_down",
    )(sched, n_used, hid, w_down, b_down)


def _combine_kernel(dest_ref, ys_hbm, x1_ref, rw_ref, g2_ref, o_ref, buf, sem):
    i = pl.program_id(0)
    n = pl.num_programs(0)
    tm = TM_COMB
    rows = TOP_K * tm

    @pl.when(i == 0)
    def _():
        _issue_row_gather(dest_ref, 0, rows, ys_hbm, buf.at[0], sem.at[0])

    @pl.when(i + 1 < n)
    def _():
        nxt = (i + 1) % 2
        _issue_row_gather(dest_ref, (i + 1) * rows, rows, ys_hbm, buf.at[nxt], sem.at[nxt])

    cur = i % 2
    _wait_row_gather(rows, ys_hbm, buf.at[cur], sem.at[cur])
    rw = rw_ref[...]
    y_lo = jnp.zeros((tm, HALF), F32)
    y_hi = jnp.zeros((tm, HALF), F32)
    for k in range(TOP_K):
        lo, hi = _unpack_rows(_load_row_slabs(buf.at[cur], k * tm, tm))
        y_lo = y_lo + rw[:, k:k + 1] * lo
        y_hi = y_hi + rw[:, k:k + 1] * hi
    o_ref[...] = x1_ref[...] + g2_ref[0] * jnp.concatenate([y_lo, y_hi], axis=-1)


def _combine(dest_km, ys, x1, rw, g2, seq):
    t, d = x1.shape
    tm = TM_COMB
    per_seq = seq // tm
    return pl.pallas_call(
        _combine_kernel,
        grid_spec=pltpu.PrefetchScalarGridSpec(
            num_scalar_prefetch=1,
            grid=(t // tm,),
            in_specs=[pl.BlockSpec(memory_space=pl.ANY),
                      pl.BlockSpec((tm, d), lambda i, dr: (i, 0)),
                      pl.BlockSpec((tm, LANES), lambda i, dr: (i, 0)),
                      pl.BlockSpec((1, 1, d), lambda i, dr: (i // per_seq, 0, 0))],
            out_specs=pl.BlockSpec((tm, d), lambda i, dr: (i, 0)),
            scratch_shapes=[pltpu.VMEM((2, TOP_K * tm * ROW_SLAB, LANES), U32),
                            pltpu.SemaphoreType.DMA((2,))]),
        out_shape=jax.ShapeDtypeStruct((t, d), F32),
        compiler_params=_params(1),
        name="combine",
    )(dest_km, ys, x1, rw, g2)


def _pad_cols(w, n):
    return jnp.pad(w, ((0, 0), (0, n - w.shape[1])))


def _layer(x, mod, norm_mix_g, norm_ffn_g, w_in, b_f, q_norm_g, k_norm_g, conv_w, conv_b,
           conv_ln_g, conv_ln_b, w_out, w_router, b_router, w_gate, b_gate, w_up, b_up,
           w_down, b_down):
    b, s, d = x.shape
    t = b * s
    shift1, scale1, gate1, shift2, scale2, gate2 = [m[:, None, :] for m in jnp.split(mod, 6, axis=-1)]
    a1 = norm_mix_g[None, None, :] * (1.0 + scale1)
    a2 = norm_ffn_g[None, None, :] * (1.0 + scale2)

    aw = ATTN_WIDTH
    w_cat = jnp.concatenate([w_in[:, :3 * aw], w_in[:, 3 * aw + ATTN_HEADS:]], axis=1).astype(BF16)
    w_f = _pad_cols(w_in[:, 3 * aw:3 * aw + ATTN_HEADS], LANES).astype(BF16)
    b_f_pad = _pad_cols(b_f[None, :], LANES)

    q, k, v, u, logf = _in_proj(x, a1, shift1, w_cat, w_f, b_f_pad,
                                q_norm_g[None, :], k_norm_g[None, :])
    attn = _fox_attention(q, k, v, _forget_cumsum(logf))
    conv = _conv_module(u, conv_w, conv_b[None, :], conv_ln_g[None, :], conv_ln_b[None, :])

    wr = _pad_cols(w_router, LANES)
    wr_hi = wr.astype(BF16)
    wr_lo = (wr - wr_hi.astype(F32)).astype(BF16)
    b_r = jnp.concatenate([b_router, jnp.full((LANES - N_EXPERTS,), NEG_BIG, F32)])[None, :]
    x1, z, logits = _out_proj(attn, conv, x, w_out.astype(BF16), gate1, a2, shift2, wr_hi, wr_lo, b_r)

    ri, rw, cnt = _route(logits.reshape(t, LANES))

    tm = TM_MOE
    n_blocks = t * TOP_K // tm + N_EXPERTS
    idx = ri[:, 0:TOP_K]
    rank = ri[:, TOP_K:2 * TOP_K]
    counts = cnt[0, :N_EXPERTS].astype(I32)
    padded = (counts + tm - 1) // tm * tm
    padded_end = jnp.cumsum(padded)
    padded_start = padded_end - padded
    dest = padded_start[idx] + rank
    block_start = jnp.arange(n_blocks, dtype=I32)[:, None] * tm
    block_e = jnp.minimum(jnp.sum((padded_end[None, :] <= block_start).astype(I32), axis=1), N_EXPERTS - 1)
    n_used = (padded_end[-1:] // tm).astype(I32)
    dest_km = dest.reshape(t // TM_COMB, TM_COMB, TOP_K).transpose(0, 2, 1).reshape(-1)

    pad_table = jnp.concatenate([padded_start + counts, padded - counts, n_used])
    xs = _dispatch(dest.reshape(-1), pad_table, z, n_blocks * tm)

    experts = jnp.arange(N_EXPERTS, dtype=I32)
    has = counts > 0
    later = (experts[None, :] > experts[:, None]) & has[None, :]
    next_e = jnp.min(jnp.where(later, experts[None, :], N_EXPERTS), axis=1)
    run_slot = (jnp.cumsum(has.astype(I32)) - 1) % 2
    run_len = jnp.maximum(padded // tm, 1)[block_e]
    pos = jnp.arange(n_blocks, dtype=I32) - (padded_start // tm)[block_e]
    nxt = next_e[block_e]
    live = nxt < N_EXPERTS
    c0 = jnp.where(live, N_WCHUNK * pos // run_len, 0)
    c1 = jnp.where(live, N_WCHUNK * (pos + 1) // run_len, 0)
    sched = jnp.stack([block_e, run_slot[block_e], jnp.where(live, nxt, 0), c0, c1], axis=1).reshape(-1)

    hid = _moe_up(sched, n_used, xs, w_gate, b_gate[:, None, :], w_up, b_up[:, None, :])
    ys = _moe_down(sched, n_used, hid, w_down, b_down[:, None, :])
    out = _combine(dest_km, ys, x1.reshape(t, d), rw, gate2, s)
    return out.reshape(b, s, d)


def kernel(x, c, ada_w, ada_b, norm_mix_g, norm_ffn_g, w_in, b_f, q_norm_g, k_norm_g, conv_w, conv_b,
           conv_ln_g, conv_ln_b, w_out, w_router, b_router, w_gate, b_gate, w_up, b_up, w_down, b_down):
    b = x.shape[0]
    c_pad = jnp.pad(c, ((0, SUBLANES - b), (0, 0)))
    for l in range(ada_w.shape[0]):
        mod = _ada_mod(c_pad, ada_w[l], ada_b[l])[:b]
        x = _layer(x, mod, norm_mix_g[l], norm_ffn_g[l], w_in[l], b_f[l], q_norm_g[l], k_norm_g[l],
                   conv_w[l], conv_b[l], conv_ln_g[l], conv_ln_b[l], w_out[l], w_router[l],
                   b_router[l], w_gate[l], b_gate[l], w_up[l], b_up[l], w_down[l], b_down[l])
    return x
```

```python
import functools

import jax
import jax.numpy as jnp
from jax import lax
from jax.experimental import pallas as pl
from jax.experimental.pallas import tpu as pltpu

F32 = jnp.float32
BF16 = jnp.bfloat16
I32 = jnp.int32
U32 = jnp.uint32

D_MODEL = 2048
ATTN_HEADS = 8
HEAD_DIM = 128
ATTN_WIDTH = ATTN_HEADS * HEAD_DIM
CONV_WIDTH = D_MODEL - ATTN_WIDTH
CONV_KERNEL = 31
N_EXPERTS = 32
TOP_K = 4
SWIGLU_LIMIT = 7.0
SWIGLU_ALPHA = 1.702
EPS = 1e-6
LOG2E = 1.4426950408889634

LANES = 128
SUBLANES = 8
HALF = D_MODEL // 2
ROW_SLAB = HALF // LANES
VMEM_LIMIT = 56 * 1024 * 1024

TM_PROJ = 512
TQ = 512
ATTN_V_ROWS = HEAD_DIM + 16
CUM_CHUNK = 256
TS_CONV = 256
CONV_HALO = 32
CONV_CHUNK = 32
TM_ROUTE = 512
TM_MOE = 256
TM_DISP = 256
TM_COMB = 256
GATHER_UNROLL = 32
W_CHUNK = 256
N_WCHUNK = D_MODEL // W_CHUNK
N_STAGE = 2
NEG_BIG = -1e30


def _params(n_axes):
    return pltpu.CompilerParams(
        dimension_semantics=("arbitrary",) * n_axes, vmem_limit_bytes=VMEM_LIMIT)


def _resident(shape):
    nd = len(shape)
    return pl.BlockSpec(shape, lambda *_: (0,) * nd, pipeline_mode=pl.Buffered(1))


def _ada_kernel(c_ref, w_ref, b_ref, o_ref):
    c = c_ref[...]
    c_act = (c * jax.nn.sigmoid(c)).astype(BF16)
    o_ref[...] = jnp.dot(c_act, w_ref[...].astype(BF16), preferred_element_type=F32) + b_ref[...]


def _ada_mod(c_pad, ada_w, ada_b):
    rows, d = c_pad.shape
    n = ada_w.shape[1]
    tn = 1024
    return pl.pallas_call(
        _ada_kernel,
        grid=(n // tn,),
        in_specs=[pl.BlockSpec((rows, d), lambda j: (0, 0)),
                  pl.BlockSpec((d, tn), lambda j: (0, j)),
                  pl.BlockSpec((1, tn), lambda j: (0, j))],
        out_specs=pl.BlockSpec((rows, tn), lambda j: (0, j)),
        out_shape=jax.ShapeDtypeStruct((rows, n), F32),
        compiler_params=_params(1),
        name="ada_mod",
    )(c_pad, ada_w, ada_b.reshape(1, n))


def _log_sigmoid(x):
    return jnp.minimum(x, 0.0) - jnp.log1p(jnp.exp(-jnp.abs(x)))


def _head_rms(y, g):
    outs = []
    for h in range(ATTN_HEADS):
        yh = y[:, h * HEAD_DIM:(h + 1) * HEAD_DIM]
        r = lax.rsqrt(jnp.mean(yh * yh, axis=-1, keepdims=True) + EPS)
        outs.append(yh * r * g)
    return jnp.concatenate(outs, axis=-1)


def _inproj_kernel(x_ref, a_ref, s_ref, w_ref, wf_ref, bf_ref, qg_ref, kg_ref,
                   q_ref, k_ref, v_ref, u_ref, f_ref):
    x = x_ref[0]
    h = x * lax.rsqrt(jnp.mean(x * x, axis=-1, keepdims=True) + EPS) * a_ref[0] + s_ref[0]
    hb = h.astype(BF16)
    aw = ATTN_WIDTH
    q = jnp.dot(hb, w_ref[:, 0:aw], preferred_element_type=F32)
    q_ref[0] = (_head_rms(q, qg_ref[...]) * (LOG2E * HEAD_DIM ** -0.5)).astype(BF16)
    k = jnp.dot(hb, w_ref[:, aw:2 * aw], preferred_element_type=F32)
    k_ref[0] = _head_rms(k, kg_ref[...]).astype(BF16)
    v_ref[0] = jnp.dot(hb, w_ref[:, 2 * aw:3 * aw], preferred_element_type=F32).astype(BF16)
    a = jnp.dot(hb, w_ref[:, 3 * aw:3 * aw + CONV_WIDTH], preferred_element_type=F32)
    g = jnp.dot(hb, w_ref[:, 3 * aw + CONV_WIDTH:], preferred_element_type=F32)
    u_ref[0] = (a * jax.nn.sigmoid(g)).astype(BF16)
    fl = jnp.dot(hb, wf_ref[...], preferred_element_type=F32) + bf_ref[...]
    f_ref[0] = _log_sigmoid(fl)


def _in_proj(x, a1, s1, w_cat, w_f, b_f, q_g, k_g):
    b, s, d = x.shape
    tm = TM_PROJ
    ncat = w_cat.shape[1]
    tok = lambda w: pl.BlockSpec((1, tm, w), lambda bi, i: (bi, i, 0))
    per_batch = pl.BlockSpec((1, 1, d), lambda bi, i: (bi, 0, 0))
    return pl.pallas_call(
        _inproj_kernel,
        grid=(b, s // tm),
        in_specs=[tok(d), per_batch, per_batch,
                  _resident((d, ncat)), _resident((d, LANES)), _resident((1, LANES)),
                  _resident((1, HEAD_DIM)), _resident((1, HEAD_DIM))],
        out_specs=[tok(ATTN_WIDTH), tok(ATTN_WIDTH), tok(ATTN_WIDTH), tok(CONV_WIDTH), tok(LANES)],
        out_shape=[jax.ShapeDtypeStruct((b, s, ATTN_WIDTH), BF16)] * 3
        + [jax.ShapeDtypeStruct((b, s, CONV_WIDTH), BF16),
           jax.ShapeDtypeStruct((b, s, LANES), F32)],
        compiler_params=_params(2),
        name="in_proj",
    )(x, a1, s1, w_cat, w_f, b_f, q_g, k_g)


def _bf16_pieces(c):
    p0 = c.astype(BF16)
    r0 = c - p0.astype(F32)
    p1 = r0.astype(BF16)
    p2 = (r0 - p1.astype(F32)).astype(BF16)
    return p0, p1, p2


def _cumsum_kernel(f_ref, c_ref):
    ch = CUM_CHUNK
    s = f_ref.shape[1]
    row = lax.broadcasted_iota(I32, (ch, ch), 0)
    col = lax.broadcasted_iota(I32, (ch, ch), 1)
    tri = (col <= row).astype(BF16)
    carry = jnp.zeros((1, LANES), F32)
    for i in range(s // ch):
        cs = carry
        for p in _bf16_pieces(f_ref[0, i * ch:(i + 1) * ch, :]):
            cs = cs + jnp.dot(tri, p, preferred_element_type=F32)
        c_ref[0, i * ch:(i + 1) * ch, :] = cs
        carry = cs[ch - 1:ch, :]


def _forget_cumsum(logf):
    b, s, _ = logf.shape
    return pl.pallas_call(
        _cumsum_kernel,
        grid=(b,),
        in_specs=[pl.BlockSpec((1, s, LANES), lambda bi: (bi, 0, 0))],
        out_specs=pl.BlockSpec((1, s, LANES), lambda bi: (bi, 0, 0)),
        out_shape=jax.ShapeDtypeStruct((b, s, LANES), F32),
        compiler_params=_params(1),
        name="forget_cumsum",
    )(logf)


def _attn_kernel(q_ref, k_ref, v_ref, cq_ref, ck_ref, o_ref, kx_sc, vt_sc, s_sc, m_sc, acc_sc):
    h = pl.program_id(1)
    qi = pl.program_id(2)
    tq = TQ
    n_kv = k_ref.shape[1] // tq
    sel_r = lax.broadcasted_iota(I32, (LANES, LANES), 0)
    sel_c = lax.broadcasted_iota(I32, (LANES, LANES), 1)
    lane1 = lax.broadcasted_iota(I32, (1, LANES), 1)

    def spread(c, first_lane, sign):
        out = jnp.zeros(c.shape, F32)
        for i, p in enumerate(_bf16_pieces(c * LOG2E)):
            sel = jnp.where((sel_r == h) & (sel_c == first_lane + i), sign, 0.0).astype(BF16)
            out = out + jnp.dot(p, sel, preferred_element_type=F32)
        return out

    @pl.when(qi == 0)
    def _():
        ones_k = jnp.where((lane1 >= 3) & (lane1 < 6), 1.0, 0.0)
        extra = lax.broadcasted_iota(I32, (ATTN_V_ROWS - HEAD_DIM, tq), 0)
        ones_row = jnp.where(extra == 0, 1.0, 0.0).astype(BF16)

        def fill(i, carry):
            st = pl.multiple_of(i * tq, tq)
            kx_sc[pl.ds(st, tq), :] = (spread(ck_ref[0, pl.ds(st, tq), :], 0, -1.0) + ones_k).astype(BF16)
            vt_sc[0:HEAD_DIM, pl.ds(st, tq)] = v_ref[0, pl.ds(st, tq), :].astype(F32).T.astype(BF16)
            vt_sc[HEAD_DIM:, pl.ds(st, tq)] = ones_row
            return carry

        lax.fori_loop(0, n_kv, fill, 0)

    ones_q = jnp.where(lane1 < 3, 1.0, 0.0)
    qx = (spread(cq_ref[0], 3, 1.0) + ones_q).astype(BF16)
    q_aug = jnp.concatenate([q_ref[0], qx], axis=-1)

    m_sc[...] = jnp.full(m_sc.shape, -jnp.inf, F32)
    acc_sc[...] = jnp.zeros(acc_sc.shape, F32)

    def scores_into(j, slot):
        st = pl.multiple_of(j * tq, tq)
        k_aug = jnp.concatenate([k_ref[0, pl.ds(st, tq), :], kx_sc[pl.ds(st, tq), :]], axis=-1)
        s_sc[slot] = lax.dot_general(k_aug, q_aug, (((1,), (1,)), ((), ())),
                                     preferred_element_type=F32)

    def update(j, slot, masked):
        st = pl.multiple_of(j * tq, tq)
        s = s_sc[slot]
        if masked:
            key = lax.broadcasted_iota(I32, (tq, tq), 0)
            qry = lax.broadcasted_iota(I32, (tq, tq), 1)
            s = jnp.where(key <= qry, s, -jnp.inf)
        m_prev = m_sc[...]
        m_new = jnp.maximum(m_prev, jnp.max(s, axis=0, keepdims=True))
        alpha = jnp.exp2(m_prev - m_new)
        p = jnp.exp2(s - m_new)
        pv = jnp.dot(vt_sc[:, pl.ds(st, tq)], p.astype(BF16), preferred_element_type=F32)
        acc_sc[...] = alpha * acc_sc[...] + pv
        m_sc[...] = m_new

    scores_into(0, 0)

    def body(jj, carry):
        j = 2 * jj
        scores_into(j + 1, 1)
        update(j, 0, False)
        scores_into(j + 2, 0)
        update(j + 1, 1, False)
        return carry

    lax.fori_loop(0, qi // 2, body, 0)

    @pl.when(qi % 2 == 1)
    def _():
        scores_into(qi, 1)
        update(qi - 1, 0, False)
        update(qi, 1, True)

    @pl.when(qi % 2 == 0)
    def _():
        update(qi, 0, True)

    acc = acc_sc[...]
    o_ref[0] = (acc[0:HEAD_DIM] / acc[HEAD_DIM:HEAD_DIM + 1]).T.astype(BF16)


def _fox_attention(q, k, v, cum):
    b, s, _ = q.shape
    tq = TQ
    return pl.pallas_call(
        _attn_kernel,
        grid=(b, ATTN_HEADS, s // tq),
        in_specs=[pl.BlockSpec((1, tq, HEAD_DIM), lambda bi, h, i: (bi, i, h)),
                  pl.BlockSpec((1, s, HEAD_DIM), lambda bi, h, i: (bi, 0, h)),
                  pl.BlockSpec((1, s, HEAD_DIM), lambda bi, h, i: (bi, 0, h)),
                  pl.BlockSpec((1, tq, LANES), lambda bi, h, i: (bi, i, 0)),
                  pl.BlockSpec((1, s, LANES), lambda bi, h, i: (bi, 0, 0))],
        out_specs=pl.BlockSpec((1, tq, HEAD_DIM), lambda bi, h, i: (bi, i, h)),
        out_shape=jax.ShapeDtypeStruct((b, s, ATTN_WIDTH), BF16),
        scratch_shapes=[pltpu.VMEM((s, LANES), BF16),
                        pltpu.VMEM((ATTN_V_ROWS, s), BF16),
                        pltpu.VMEM((2, tq, tq), F32),
                        pltpu.VMEM((1, tq), F32),
                        pltpu.VMEM((ATTN_V_ROWS, tq), F32)],
        compiler_params=_params(3),
        name="fox_attention",
    )(q, k, v, cum, cum)


def _conv_kernel(u_ref, w_ref, cb_ref, lg_ref, lb_ref, o_ref, ubuf, shifted, wb):
    i = pl.program_id(1)
    ts, halo, ck = TS_CONV, CONV_HALO, CONV_CHUNK

    @pl.when(i == 0)
    def _():
        ubuf[0:halo, :] = jnp.zeros((halo, CONV_WIDTH), F32)
        for j in range(CONV_KERNEL):
            wb[j * SUBLANES:(j + 1) * SUBLANES, :] = jnp.broadcast_to(
                w_ref[j:j + 1, :], (SUBLANES, CONV_WIDTH))

    ubuf[halo:halo + ts, :] = u_ref[0].astype(F32)
    base = halo - (CONV_KERNEL - 1)
    span = shifted.shape[1]
    for r in range(1, SUBLANES):
        shifted[r] = ubuf[r:r + span, :]
    for c in range(ts // ck):
        acc = jnp.zeros((ck, CONV_WIDTH), F32)
        for j in range(CONV_KERNEL):
            off = c * ck + base + j
            r, al = off % SUBLANES, off - off % SUBLANES
            slab = ubuf[al:al + ck, :] if r == 0 else shifted[r, al:al + ck, :]
            wj = wb[j * SUBLANES:(j + 1) * SUBLANES, :]
            acc = acc + slab * jnp.concatenate([wj] * (ck // SUBLANES), axis=0)
        y = acc + cb_ref[...]
        mu = jnp.mean(y, axis=-1, keepdims=True)
        yc = y - mu
        var = jnp.mean(yc * yc, axis=-1, keepdims=True)
        z = yc * lax.rsqrt(var + EPS) * lg_ref[...] + lb_ref[...]
        o_ref[0, c * ck:(c + 1) * ck, :] = (z * jax.nn.sigmoid(z)).astype(BF16)
    ubuf[0:halo, :] = ubuf[ts:ts + halo, :]


def _conv_module(u, conv_w, conv_b, ln_g, ln_b):
    b, s, cw = u.shape
    ts = TS_CONV
    return pl.pallas_call(
        _conv_kernel,
        grid=(b, s // ts),
        in_specs=[pl.BlockSpec((1, ts, cw), lambda bi, i: (bi, i, 0)),
                  _resident((CONV_KERNEL, cw)), _resident((1, cw)),
                  _resident((1, cw)), _resident((1, cw))],
        out_specs=pl.BlockSpec((1, ts, cw), lambda bi, i: (bi, i, 0)),
        out_shape=jax.ShapeDtypeStruct((b, s, cw), BF16),
        scratch_shapes=[pltpu.VMEM((CONV_HALO + ts, cw), F32),
                        pltpu.VMEM((SUBLANES, ts + CONV_HALO - SUBLANES, cw), F32),
                        pltpu.VMEM((CONV_KERNEL * SUBLANES, cw), F32)],
        compiler_params=_params(2),
        name="conv_module",
    )(u, conv_w, conv_b, ln_g, ln_b)


def _pack_rows(val):
    lo = lax.bitcast_convert_type(val[:, :HALF].astype(BF16).astype(F32), U32)
    hi = lax.bitcast_convert_type(val[:, HALF:].astype(BF16).astype(F32), U32)
    return hi | (lo >> 16)


def _unpack_rows(word):
    lo = lax.bitcast_convert_type(word << 16, F32)
    hi = lax.bitcast_convert_type(word & jnp.uint32(0xFFFF0000), F32)
    return lo, hi


def _store_row_slabs(ref, word):
    rows = word.shape[0]
    for s in range(ROW_SLAB):
        ref[pl.ds(s, rows, stride=ROW_SLAB), :] = word[:, s * LANES:(s + 1) * LANES]


def _load_row_slabs(ref, start, rows):
    parts = [ref[pl.ds(start * ROW_SLAB + s, rows, stride=ROW_SLAB), :] for s in range(ROW_SLAB)]
    return jnp.concatenate(parts, axis=-1)


def _outproj_kernel(at_ref, cv_ref, x_ref, wo_ref, g1_ref, a2_ref, s2_ref, wrh_ref, wrl_ref, br_ref,
                    x1_ref, z_ref, lg_ref):
    mix = (jnp.dot(at_ref[0], wo_ref[0:ATTN_WIDTH, :], preferred_element_type=F32)
           + jnp.dot(cv_ref[0], wo_ref[ATTN_WIDTH:, :], preferred_element_type=F32))
    x1 = x_ref[0] + g1_ref[0] * mix
    x1_ref[0] = x1
    h2 = x1 * lax.rsqrt(jnp.mean(x1 * x1, axis=-1, keepdims=True) + EPS) * a2_ref[0] + s2_ref[0]
    _store_row_slabs(z_ref, _pack_rows(h2))
    hi = h2.astype(BF16)
    lo = (h2 - hi.astype(F32)).astype(BF16)
    lg_ref[0] = (jnp.dot(hi, wrh_ref[...], preferred_element_type=F32)
                 + jnp.dot(lo, wrh_ref[...], preferred_element_type=F32)
                 + jnp.dot(hi, wrl_ref[...], preferred_element_type=F32)) + br_ref[...]


def _out_proj(attn, conv, x, w_out, g1, a2, s2, wr_hi, wr_lo, b_r):
    b, s, d = x.shape
    tm = TM_PROJ
    nt = s // tm
    tok = lambda w: pl.BlockSpec((1, tm, w), lambda bi, i: (bi, i, 0))
    per_batch = pl.BlockSpec((1, 1, d), lambda bi, i: (bi, 0, 0))
    return pl.pallas_call(
        _outproj_kernel,
        grid=(b, nt),
        in_specs=[tok(ATTN_WIDTH), tok(CONV_WIDTH), tok(d), _resident((d, d)),
                  per_batch, per_batch, per_batch,
                  _resident((d, LANES)), _resident((d, LANES)), _resident((1, LANES))],
        out_specs=[tok(d),
                   pl.BlockSpec((tm * ROW_SLAB, LANES), lambda bi, i: (bi * nt + i, 0)),
                   tok(LANES)],
        out_shape=[jax.ShapeDtypeStruct((b, s, d), F32),
                   jax.ShapeDtypeStruct((b * s * ROW_SLAB, LANES), U32),
                   jax.ShapeDtypeStruct((b, s, LANES), F32)],
        compiler_params=_params(2),
        name="out_proj",
    )(attn, conv, x, w_out, g1, a2, s2, wr_hi, wr_lo, b_r)


def _route_kernel(lg_ref, ri_ref, rw_ref, cnt_ref, carry):
    i = pl.program_id(0)
    tm = TM_ROUTE

    @pl.when(i == 0)
    def _():
        carry[...] = jnp.zeros(carry.shape, F32)

    l = lg_ref[...]
    lane = lax.broadcasted_iota(I32, (tm, LANES), 1).astype(F32)
    vals, idxs = [], []
    for _ in range(TOP_K):
        m = jnp.max(l, axis=-1, keepdims=True)
        ix = jnp.min(jnp.where(l == m, lane, float(LANES)), axis=-1, keepdims=True)
        vals.append(m)
        idxs.append(ix)
        l = jnp.where(lane == ix, -jnp.inf, l)
    es = [jnp.exp(v - vals[0]) for v in vals]
    den = es[0] + es[1] + es[2] + es[3]
    onehot = jnp.zeros((tm, LANES), F32)
    for ix in idxs:
        onehot = onehot + jnp.where(lane == ix, 1.0, 0.0)
    row = lax.broadcasted_iota(I32, (tm, tm), 0)
    col = lax.broadcasted_iota(I32, (tm, tm), 1)
    strict = (col < row).astype(BF16)
    before = jnp.dot(strict, onehot.astype(BF16), preferred_element_type=F32) + carry[0:1, :]
    ri = jnp.zeros((tm, LANES), F32)
    rw = jnp.zeros((tm, LANES), F32)
    for k in range(TOP_K):
        rank = jnp.sum(jnp.where(lane == idxs[k], before, 0.0), axis=-1, keepdims=True)
        ri = jnp.where(lane == k, idxs[k], ri)
        ri = jnp.where(lane == TOP_K + k, rank, ri)
        rw = jnp.where(lane == k, es[k] / den, rw)
    ri_ref[...] = ri.astype(I32)
    rw_ref[...] = rw
    carry[0:1, :] = carry[0:1, :] + jnp.sum(onehot, axis=0, keepdims=True)
    cnt_ref[...] = carry[...]


def _route(logits):
    t = logits.shape[0]
    tm = TM_ROUTE
    return pl.pallas_call(
        _route_kernel,
        grid=(t // tm,),
        in_specs=[pl.BlockSpec((tm, LANES), lambda i: (i, 0))],
        out_specs=[pl.BlockSpec((tm, LANES), lambda i: (i, 0)),
                   pl.BlockSpec((tm, LANES), lambda i: (i, 0)),
                   pl.BlockSpec((SUBLANES, LANES), lambda i: (0, 0))],
        out_shape=[jax.ShapeDtypeStruct((t, LANES), I32),
                   jax.ShapeDtypeStruct((t, LANES), F32),
                   jax.ShapeDtypeStruct((SUBLANES, LANES), F32)],
        scratch_shapes=[pltpu.VMEM((SUBLANES, LANES), F32)],
        compiler_params=_params(1),
        name="route",
    )(logits)


def _issue_row_gather(idx_ref, idx_base, n_rows, src_hbm, dst_buf, sem):
    def body(g, carry):
        for u in range(GATHER_UNROLL):
            r = g * GATHER_UNROLL + u
            t = idx_ref[idx_base + r]
            pltpu.make_async_copy(
                src_hbm.at[pl.ds(pl.multiple_of(t * ROW_SLAB, ROW_SLAB), ROW_SLAB), :],
                dst_buf.at[pl.ds(pl.multiple_of(r * ROW_SLAB, ROW_SLAB), ROW_SLAB), :],
                sem).start(priority=u % 2)
        return carry
    lax.fori_loop(0, n_rows // GATHER_UNROLL, body, 0)


def _wait_row_gather(n_rows, src_hbm, dst_buf, sem):
    pltpu.make_async_copy(src_hbm.at[pl.ds(0, n_rows * ROW_SLAB), :], dst_buf, sem).wait()


def _dispatch_kernel(dest_ref, pad_ref, z_ref, xs_hbm, stage, zeros, sem, zsem):
    i = pl.program_id(0)
    n = pl.num_programs(0)
    tm = TM_DISP
    pad_rows = TM_MOE * ROW_SLAB

    def zero_copy(slot, n_slots):
        start = pl.multiple_of(slot * ROW_SLAB, ROW_SLAB)
        return pltpu.make_async_copy(zeros.at[pl.ds(0, n_slots * ROW_SLAB), :],
                                     xs_hbm.at[pl.ds(start, n_slots * ROW_SLAB), :], zsem)

    def for_pad_pieces(e, fn):
        slot = pad_ref[e]
        n = pad_ref[N_EXPERTS + e]
        size = TM_MOE // 2
        while size >= 1:
            piece = zero_copy(slot, size)
            pl.when((n & size) != 0)(functools.partial(fn, piece))
            slot = slot + (n & size)
            size //= 2

    @pl.when(i == 0)
    def _():
        zeros[...] = jnp.zeros(zeros.shape, U32)
        first_free = pad_ref[2 * N_EXPERTS]
        n_total = xs_hbm.shape[0] // pad_rows

        def start_all(e, carry):
            for_pad_pieces(e, lambda piece: piece.start())
            return carry

        def wait_all(e, carry):
            for_pad_pieces(e, lambda piece: piece.wait())
            return carry

        def start_free(b, carry):
            zero_copy(b * TM_MOE, TM_MOE).start()
            return carry

        def wait_free(b, carry):
            zero_copy(b * TM_MOE, TM_MOE).wait()
            return carry

        lax.fori_loop(0, N_EXPERTS, start_all, 0)
        lax.fori_loop(first_free, n_total, start_free, 0)
        lax.fori_loop(0, N_EXPERTS, wait_all, 0)
        lax.fori_loop(first_free, n_total, wait_free, 0)

    cur = i % 2
    stage[cur] = z_ref[...]

    def body(g, carry):
        for u in range(GATHER_UNROLL // TOP_K):
            r = g * (GATHER_UNROLL // TOP_K) + u
            src = stage.at[cur, pl.ds(pl.multiple_of(r * ROW_SLAB, ROW_SLAB), ROW_SLAB), :]
            for k in range(TOP_K):
                d = dest_ref[(i * tm + r) * TOP_K + k]
                pltpu.make_async_copy(
                    src, xs_hbm.at[pl.ds(pl.multiple_of(d * ROW_SLAB, ROW_SLAB), ROW_SLAB), :],
                    sem.at[cur]).start(priority=k % 2)
        return carry

    lax.fori_loop(0, tm * TOP_K // GATHER_UNROLL, body, 0)

    def drain(slot):
        for _ in range(TOP_K):
            pltpu.make_async_copy(stage.at[slot], xs_hbm.at[pl.ds(0, tm * ROW_SLAB), :], sem.at[slot]).wait()

    @pl.when(i > 0)
    def _():
        drain(1 - cur)

    @pl.when(i == n - 1)
    def _():
        drain(cur)


def _dispatch(dest_flat, pad_start, z, n_slots):
    t = z.shape[0] // ROW_SLAB
    tm = TM_DISP
    return pl.pallas_call(
        _dispatch_kernel,
        grid_spec=pltpu.PrefetchScalarGridSpec(
            num_scalar_prefetch=2,
            grid=(t // tm,),
            in_specs=[pl.BlockSpec((tm * ROW_SLAB, LANES), lambda i, dr, pr: (i, 0))],
            out_specs=pl.BlockSpec(memory_space=pl.ANY),
            scratch_shapes=[pltpu.VMEM((2, tm * ROW_SLAB, LANES), U32),
                            pltpu.VMEM((TM_MOE * ROW_SLAB, LANES), U32),
                            pltpu.SemaphoreType.DMA((2,)), pltpu.SemaphoreType.DMA(())]),
        out_shape=jax.ShapeDtypeStruct((n_slots * ROW_SLAB, LANES), U32),
        compiler_params=_params(1),
        name="dispatch",
    )(dest_flat, pad_start, z)


SCHED_W = 5
S_EXPERT, S_SLOT, S_NEXT, S_C0, S_C1 = range(SCHED_W)


def _expert_weights_step(sched_ref, b, n_used, w_hbm, wbuf, stage, sems, compute):
    n_mat = len(w_hbm)
    n_chunk = wbuf.shape[2] // W_CHUNK

    def rows(rc):
        return pl.ds(pl.multiple_of(rc * W_CHUNK, W_CHUNK), W_CHUNK)

    def copies(e, rc, j):
        return [pltpu.make_async_copy(w_hbm[m].at[e, rows(rc), :], stage.at[j, m], sems.at[j, m])
                for m in range(n_mat)]

    def convert(slot, rc, j):
        for m in range(n_mat):
            wbuf[slot, m, rows(rc), :] = stage[j, m].astype(BF16)

    def load_now(e, slot, rc0, rc1):
        def body(rc, carry):
            for cp in copies(e, rc, 0):
                cp.start()
            for cp in copies(e, rc, 0):
                cp.wait()
            convert(slot, rc, 0)
            return carry
        lax.fori_loop(rc0, rc1, body, 0)

    @pl.when(b == 0)
    def _():
        load_now(sched_ref[S_EXPERT], 0, 0, n_chunk)

    @pl.when(b < n_used)
    def _():
        base = b * SCHED_W
        slot, nxt = sched_ref[base + S_SLOT], sched_ref[base + S_NEXT]
        c0, c1 = sched_ref[base + S_C0], sched_ref[base + S_C1]
        def start_chunk(c, carry):
            for cp in copies(nxt, c, c - c0):
                cp.start()
            return carry
        lax.fori_loop(c0, jnp.minimum(c0 + N_STAGE, c1), start_chunk, 0)
        compute(slot)
        for j in range(N_STAGE):
            @pl.when(c0 + j < c1)
            def _():
                for cp in copies(nxt, c0 + j, j):
                    cp.wait()
                convert(1 - slot, c0 + j, j)
        load_now(nxt, 1 - slot, jnp.minimum(c0 + N_STAGE, c1), c1)


def _moe_up_kernel(sched_ref, nu_ref, xs_ref, wg_hbm, bg_ref, wu_hbm, bu_ref, hid_ref, wbuf, stage, sems):
    b = pl.program_id(0)
    tm = TM_MOE

    def compute(slot):
        lo, hi = _unpack_rows(_load_row_slabs(xs_ref, 0, tm))
        x = jnp.concatenate([lo.astype(BF16), hi.astype(BF16)], axis=-1)
        g = jnp.dot(x, wbuf[slot, 0], preferred_element_type=F32) + bg_ref[0]
        u = jnp.dot(x, wbuf[slot, 1], preferred_element_type=F32) + bu_ref[0]
        g = jnp.minimum(g, SWIGLU_LIMIT)
        u = jnp.clip(u, -SWIGLU_LIMIT, SWIGLU_LIMIT)
        hid_ref[...] = ((u + 1.0) * (g * jax.nn.sigmoid(SWIGLU_ALPHA * g))).astype(BF16)

    _expert_weights_step(sched_ref, b, nu_ref[0], (wg_hbm, wu_hbm), wbuf, stage, sems, compute)

    @pl.when(b >= nu_ref[0])
    def _():
        hid_ref[...] = jnp.zeros(hid_ref.shape, BF16)


def _expert_scratch(n_mat, k, n):
    return [pltpu.VMEM((2, n_mat, k, n), BF16),
            pltpu.VMEM((N_STAGE, n_mat, W_CHUNK, n), F32),
            pltpu.SemaphoreType.DMA((N_STAGE, n_mat))]


def _moe_up(sched, n_used, xs, w_gate, b_gate, w_up, b_up):
    e, d, f = w_gate.shape
    n_blocks = sched.shape[0] // SCHED_W
    tm = TM_MOE
    hbm = pl.BlockSpec(memory_space=pl.ANY)
    bspec = pl.BlockSpec((1, 1, f), lambda b, sc, nu: (sc[b * SCHED_W + S_EXPERT], 0, 0))
    return pl.pallas_call(
        _moe_up_kernel,
        grid_spec=pltpu.PrefetchScalarGridSpec(
            num_scalar_prefetch=2,
            grid=(n_blocks,),
            in_specs=[pl.BlockSpec((tm * ROW_SLAB, LANES), lambda b, sc, nu: (jnp.minimum(b, nu[0] - 1), 0)),
                      hbm, bspec, hbm, bspec],
            out_specs=pl.BlockSpec((tm, f), lambda b, sc, nu: (b, 0)),
            scratch_shapes=_expert_scratch(2, d, f)),
        out_shape=jax.ShapeDtypeStruct((n_blocks * tm, f), BF16),
        compiler_params=_params(1),
        name="moe_up",
    )(sched, n_used, xs, w_gate, b_gate, w_up, b_up)


def _moe_down_kernel(sched_ref, nu_ref, hid_ref, wd_hbm, bd_ref, ys_ref, wbuf, stage, sems):
    b = pl.program_id(0)

    def compute(slot):
        out = jnp.dot(hid_ref[...], wbuf[slot, 0], preferred_element_type=F32) + bd_ref[0]
        _store_row_slabs(ys_ref, _pack_rows(out))

    _expert_weights_step(sched_ref, b, nu_ref[0], (wd_hbm,), wbuf, stage, sems, compute)

    @pl.when(b >= nu_ref[0])
    def _():
        ys_ref[...] = jnp.zeros(ys_ref.shape, U32)


def _moe_down(sched, n_used, hid, w_down, b_down):
    e, f, d = w_down.shape
    n_blocks = sched.shape[0] // SCHED_W
    tm = TM_MOE
    return pl.pallas_call(
        _moe_down_kernel,
        grid_spec=pltpu.PrefetchScalarGridSpec(
            num_scalar_prefetch=2,
            grid=(n_blocks,),
            in_specs=[pl.BlockSpec((tm, f), lambda b, sc, nu: (b, 0)),
                      pl.BlockSpec(memory_space=pl.ANY),
                      pl.BlockSpec((1, 1, d), lambda b, sc, nu: (sc[b * SCHED_W + S_EXPERT], 0, 0))],
            out_specs=pl.BlockSpec((tm * ROW_SLAB, LANES), lambda b, sc, nu: (b, 0)),
            scratch_shapes=_expert_scratch(1, f, d)),
        out_shape=jax.ShapeDtypeStruct((n_blocks * tm * ROW_SLAB, LANES), U32),
        compiler_params=_params(1),
        name="moe_down",
    )(sched, n_used, hid, w_down, b_down)


def _combine_kernel(dest_ref, ys_hbm, x1_ref, rw_ref, g2_ref, o_ref, buf, sem):
    i = pl.program_id(0)
    n = pl.num_programs(0)
    tm = TM_COMB
    rows = TOP_K * tm

    @pl.when(i == 0)
    def _():
        _issue_row_gather(dest_ref, 0, rows, ys_hbm, buf.at[0], sem.at[0])

    @pl.when(i + 1 < n)
    def _():
        nxt = (i + 1) % 2
        _issue_row_gather(dest_ref, (i + 1) * rows, rows, ys_hbm, buf.at[nxt], sem.at[nxt])

    cur = i % 2
    _wait_row_gather(rows, ys_hbm, buf.at[cur], sem.at[cur])
    rw = rw_ref[...]
    y_lo = jnp.zeros((tm, HALF), F32)
    y_hi = jnp.zeros((tm, HALF), F32)
    for k in range(TOP_K):
        lo, hi = _unpack_rows(_load_row_slabs(buf.at[cur], k * tm, tm))
        y_lo = y_lo + rw[:, k:k + 1] * lo
        y_hi = y_hi + rw[:, k:k + 1] * hi
    o_ref[...] = x1_ref[...] + g2_ref[0] * jnp.concatenate([y_lo, y_hi], axis=-1)


def _combine(dest_km, ys, x1, rw, g2, seq):
    t, d = x1.shape
    tm = TM_COMB
    per_seq = seq // tm
    return pl.pallas_call(
        _combine_kernel,
        grid_spec=pltpu.PrefetchScalarGridSpec(
            num_scalar_prefetch=1,
            grid=(t // tm,),
            in_specs=[pl.BlockSpec(memory_space=pl.ANY),
                      pl.BlockSpec((tm, d), lambda i, dr: (i, 0)),
                      pl.BlockSpec((tm, LANES), lambda i, dr: (i, 0)),
                      pl.BlockSpec((1, 1, d), lambda i, dr: (i // per_seq, 0, 0))],
            out_specs=pl.BlockSpec((tm, d), lambda i, dr: (i, 0)),
            scratch_shapes=[pltpu.VMEM((2, TOP_K * tm * ROW_SLAB, LANES), U32),
                            pltpu.SemaphoreType.DMA((2,))]),
        out_shape=jax.ShapeDtypeStruct((t, d), F32),
        compiler_params=_params(1),
        name="combine",
    )(dest_km, ys, x1, rw, g2)


def _pad_cols(w, n):
    return jnp.pad(w, ((0, 0), (0, n - w.shape[1])))


def _layer(x, mod, norm_mix_g, norm_ffn_g, w_in, b_f, q_norm_g, k_norm_g, conv_w, conv_b,
           conv_ln_g, conv_ln_b, w_out, w_router, b_router, w_gate, b_gate, w_up, b_up,
           w_down, b_down):
    b, s, d = x.shape
    t = b * s
    shift1, scale1, gate1, shift2, scale2, gate2 = [m[:, None, :] for m in jnp.split(mod, 6, axis=-1)]
    a1 = norm_mix_g[None, None, :] * (1.0 + scale1)
    a2 = norm_ffn_g[None, None, :] * (1.0 + scale2)

    aw = ATTN_WIDTH
    w_cat = jnp.concatenate([w_in[:, :3 * aw], w_in[:, 3 * aw + ATTN_HEADS:]], axis=1).astype(BF16)
    w_f = _pad_cols(w_in[:, 3 * aw:3 * aw + ATTN_HEADS], LANES).astype(BF16)
    b_f_pad = _pad_cols(b_f[None, :], LANES)

    q, k, v, u, logf = _in_proj(x, a1, shift1, w_cat, w_f, b_f_pad,
                                q_norm_g[None, :], k_norm_g[None, :])
    attn = _fox_attention(q, k, v, _forget_cumsum(logf))
    conv = _conv_module(u, conv_w, conv_b[None, :], conv_ln_g[None, :], conv_ln_b[None, :])

    wr = _pad_cols(w_router, LANES)
    wr_hi = wr.astype(BF16)
    wr_lo = (wr - wr_hi.astype(F32)).astype(BF16)
    b_r = jnp.concatenate([b_router, jnp.full((LANES - N_EXPERTS,), NEG_BIG, F32)])[None, :]
    x1, z, logits = _out_proj(attn, conv, x, w_out.astype(BF16), gate1, a2, shift2, wr_hi, wr_lo, b_r)

    ri, rw, cnt = _route(logits.reshape(t, LANES))

    tm = TM_MOE
    n_blocks = t * TOP_K // tm + N_EXPERTS
    idx = ri[:, 0:TOP_K]
    rank = ri[:, TOP_K:2 * TOP_K]
    counts = cnt[0, :N_EXPERTS].astype(I32)
    padded = (counts + tm - 1) // tm * tm
    padded_end = jnp.cumsum(padded)
    padded_start = padded_end - padded
    dest = padded_start[idx] + rank
    block_start = jnp.arange(n_blocks, dtype=I32)[:, None] * tm
    block_e = jnp.minimum(jnp.sum((padded_end[None, :] <= block_start).astype(I32), axis=1), N_EXPERTS - 1)
    n_used = (padded_end[-1:] // tm).astype(I32)
    dest_km = dest.reshape(t // TM_COMB, TM_COMB, TOP_K).transpose(0, 2, 1).reshape(-1)

    pad_table = jnp.concatenate([padded_start + counts, padded - counts, n_used])
    xs = _dispatch(dest.reshape(-1), pad_table, z, n_blocks * tm)

    experts = jnp.arange(N_EXPERTS, dtype=I32)
    has = counts > 0
    later = (experts[None, :] > experts[:, None]) & has[None, :]
    next_e = jnp.min(jnp.where(later, experts[None, :], N_EXPERTS), axis=1)
    run_slot = (jnp.cumsum(has.astype(I32)) - 1) % 2
    of_block = block_e[:, None] == experts[None, :]

    def per_block(table):
        return jnp.sum(jnp.where(of_block, table[None, :], 0), axis=1)

    run_len = per_block(jnp.maximum(padded // tm, 1))
    pos = jnp.arange(n_blocks, dtype=I32) - per_block(padded_start // tm)
    nxt = per_block(next_e)
    live = nxt < N_EXPERTS
    c0 = jnp.where(live, N_WCHUNK * pos // run_len, 0)
    c1 = jnp.where(live, N_WCHUNK * (pos + 1) // run_len, 0)
    sched = jnp.stack([block_e, per_block(run_slot), jnp.where(live, nxt, 0), c0, c1], axis=1).reshape(-1)

    hid = _moe_up(sched, n_used, xs, w_gate, b_gate[:, None, :], w_up, b_up[:, None, :])
    ys = _moe_down(sched, n_used, hid, w_down, b_down[:, None, :])
    out = _combine(dest_km, ys, x1.reshape(t, d), rw, gate2, s)
    return out.reshape(b, s, d)


def kernel(x, c, ada_w, ada_b, norm_mix_g, norm_ffn_g, w_in, b_f, q_norm_g, k_norm_g, conv_w, conv_b,
           conv_ln_g, conv_ln_b, w_out, w_router, b_router, w_gate, b_gate, w_up, b_up, w_down, b_down):
    b = x.shape[0]
    c_pad = jnp.pad(c, ((0, SUBLANES - b), (0, 0)))
    for l in range(ada_w.shape[0]):
        mod = _ada_mod(c_pad, ada_w[l], ada_b[l])[:b]
        x = _layer(x, mod, norm_mix_g[l], norm_ffn_g[l], w_in[l], b_f[l], q_norm_g[l], k_norm_g[l],
                   conv_w[l], conv_b[l], conv_ln_g[l], conv_ln_b[l], w_out[l], w_router[l],
                   b_router[l], w_gate[l], b_gate[l], w_up[l], b_up[l], w_down[l], b_down[l])
    return x
```

```python
import functools

import jax
import jax.numpy as jnp
from jax import lax
from jax.experimental import pallas as pl
from jax.experimental.pallas import tpu as pltpu

F32 = jnp.float32
BF16 = jnp.bfloat16
I32 = jnp.int32
U32 = jnp.uint32

D_MODEL = 2048
ATTN_HEADS = 8
HEAD_DIM = 128
ATTN_WIDTH = ATTN_HEADS * HEAD_DIM
CONV_WIDTH = D_MODEL - ATTN_WIDTH
CONV_KERNEL = 31
N_EXPERTS = 32
TOP_K = 4
SWIGLU_LIMIT = 7.0
SWIGLU_ALPHA = 1.702
EPS = 1e-6
LOG2E = 1.4426950408889634

LANES = 128
SUBLANES = 8
HALF = D_MODEL // 2
ROW_SLAB = HALF // LANES
VMEM_LIMIT = 56 * 1024 * 1024

TM_PROJ = 512
TQ = 512
ATTN_V_ROWS = HEAD_DIM + 16
HEADS_PER_STEP = 2
CUM_CHUNK = 256
TS_CONV = 256
CONV_HALO = 32
CONV_CHUNK = 32
TM_ROUTE = 512
TM_MOE = 256
TM_DISP = 256
TM_COMB = 256
GATHER_UNROLL = 32
W_CHUNK = 256
N_WCHUNK = D_MODEL // W_CHUNK
N_STAGE = 2
NEG_BIG = -1e30


def _params(n_axes):
    return pltpu.CompilerParams(
        dimension_semantics=("arbitrary",) * n_axes, vmem_limit_bytes=VMEM_LIMIT)


def _resident(shape):
    nd = len(shape)
    return pl.BlockSpec(shape, lambda *_: (0,) * nd, pipeline_mode=pl.Buffered(1))


def _ada_kernel(c_ref, w_ref, b_ref, o_ref):
    c = c_ref[...]
    c_act = (c * jax.nn.sigmoid(c)).astype(BF16)
    o_ref[...] = jnp.dot(c_act, w_ref[...].astype(BF16), preferred_element_type=F32) + b_ref[...]


def _ada_mod(c_pad, ada_w, ada_b):
    rows, d = c_pad.shape
    n = ada_w.shape[1]
    tn = 1024
    return pl.pallas_call(
        _ada_kernel,
        grid=(n // tn,),
        in_specs=[pl.BlockSpec((rows, d), lambda j: (0, 0)),
                  pl.BlockSpec((d, tn), lambda j: (0, j)),
                  pl.BlockSpec((1, tn), lambda j: (0, j))],
        out_specs=pl.BlockSpec((rows, tn), lambda j: (0, j)),
        out_shape=jax.ShapeDtypeStruct((rows, n), F32),
        compiler_params=_params(1),
        name="ada_mod",
    )(c_pad, ada_w, ada_b.reshape(1, n))


def _log_sigmoid(x):
    return jnp.minimum(x, 0.0) - jnp.log1p(jnp.exp(-jnp.abs(x)))


def _head_rms(y, g):
    outs = []
    for h in range(ATTN_HEADS):
        yh = y[:, h * HEAD_DIM:(h + 1) * HEAD_DIM]
        r = lax.rsqrt(jnp.mean(yh * yh, axis=-1, keepdims=True) + EPS)
        outs.append(yh * r * g)
    return jnp.concatenate(outs, axis=-1)


def _inproj_kernel(x_ref, a_ref, s_ref, w_ref, wf_ref, bf_ref, qg_ref, kg_ref,
                   q_ref, k_ref, v_ref, u_ref, f_ref):
    x = x_ref[0]
    h = x * lax.rsqrt(jnp.mean(x * x, axis=-1, keepdims=True) + EPS) * a_ref[0] + s_ref[0]
    hb = h.astype(BF16)
    aw = ATTN_WIDTH
    q = jnp.dot(hb, w_ref[:, 0:aw], preferred_element_type=F32)
    q_ref[0] = (_head_rms(q, qg_ref[...]) * (LOG2E * HEAD_DIM ** -0.5)).astype(BF16)
    k = jnp.dot(hb, w_ref[:, aw:2 * aw], preferred_element_type=F32)
    k_ref[0] = _head_rms(k, kg_ref[...]).astype(BF16)
    v_ref[0] = jnp.dot(hb, w_ref[:, 2 * aw:3 * aw], preferred_element_type=F32).astype(BF16)
    a = jnp.dot(hb, w_ref[:, 3 * aw:3 * aw + CONV_WIDTH], preferred_element_type=F32)
    g = jnp.dot(hb, w_ref[:, 3 * aw + CONV_WIDTH:], preferred_element_type=F32)
    u_ref[0] = (a * jax.nn.sigmoid(g)).astype(BF16)
    fl = jnp.dot(hb, wf_ref[...], preferred_element_type=F32) + bf_ref[...]
    f_ref[0] = _log_sigmoid(fl)


def _in_proj(x, a1, s1, w_cat, w_f, b_f, q_g, k_g):
    b, s, d = x.shape
    tm = TM_PROJ
    ncat = w_cat.shape[1]
    tok = lambda w: pl.BlockSpec((1, tm, w), lambda bi, i: (bi, i, 0))
    per_batch = pl.BlockSpec((1, 1, d), lambda bi, i: (bi, 0, 0))
    return pl.pallas_call(
        _inproj_kernel,
        grid=(b, s // tm),
        in_specs=[tok(d), per_batch, per_batch,
                  _resident((d, ncat)), _resident((d, LANES)), _resident((1, LANES)),
                  _resident((1, HEAD_DIM)), _resident((1, HEAD_DIM))],
        out_specs=[tok(ATTN_WIDTH), tok(ATTN_WIDTH), tok(ATTN_WIDTH), tok(CONV_WIDTH), tok(LANES)],
        out_shape=[jax.ShapeDtypeStruct((b, s, ATTN_WIDTH), BF16)] * 3
        + [jax.ShapeDtypeStruct((b, s, CONV_WIDTH), BF16),
           jax.ShapeDtypeStruct((b, s, LANES), F32)],
        compiler_params=_params(2),
        name="in_proj",
    )(x, a1, s1, w_cat, w_f, b_f, q_g, k_g)


def _bf16_pieces(c):
    p0 = c.astype(BF16)
    r0 = c - p0.astype(F32)
    p1 = r0.astype(BF16)
    p2 = (r0 - p1.astype(F32)).astype(BF16)
    return p0, p1, p2


def _cumsum_kernel(f_ref, c_ref):
    ch = CUM_CHUNK
    s = f_ref.shape[1]
    row = lax.broadcasted_iota(I32, (ch, ch), 0)
    col = lax.broadcasted_iota(I32, (ch, ch), 1)
    tri = (col <= row).astype(BF16)
    carry = jnp.zeros((1, LANES), F32)
    for i in range(s // ch):
        cs = carry
        for p in _bf16_pieces(f_ref[0, i * ch:(i + 1) * ch, :]):
            cs = cs + jnp.dot(tri, p, preferred_element_type=F32)
        c_ref[0, i * ch:(i + 1) * ch, :] = cs
        carry = cs[ch - 1:ch, :]


def _forget_cumsum(logf):
    b, s, _ = logf.shape
    return pl.pallas_call(
        _cumsum_kernel,
        grid=(b,),
        in_specs=[pl.BlockSpec((1, s, LANES), lambda bi: (bi, 0, 0))],
        out_specs=pl.BlockSpec((1, s, LANES), lambda bi: (bi, 0, 0)),
        out_shape=jax.ShapeDtypeStruct((b, s, LANES), F32),
        compiler_params=_params(1),
        name="forget_cumsum",
    )(logf)


def _attn_kernel(q_ref, k_ref, v_ref, c_ref, o_ref, kx_sc, qx_sc, vt_sc, s_sc, m_sc, acc_sc):
    tq = TQ
    n_t = k_ref.shape[1] // tq
    heads = range(HEADS_PER_STEP)
    sel_r = lax.broadcasted_iota(I32, (LANES, LANES), 0)
    sel_c = lax.broadcasted_iota(I32, (LANES, LANES), 1)
    lane1 = lax.broadcasted_iota(I32, (1, LANES), 1)

    def head_lanes(hh):
        return slice(hh * HEAD_DIM, (hh + 1) * HEAD_DIM)

    def spread(pieces, hh, first_lane, sign):
        head = pl.program_id(1) * HEADS_PER_STEP + hh
        out = jnp.zeros(pieces[0].shape, F32)
        for i, p in enumerate(pieces):
            sel = jnp.where((sel_r == head) & (sel_c == first_lane + i), sign, 0.0).astype(BF16)
            out = out + jnp.dot(p, sel, preferred_element_type=F32)
        return out

    ones_k = jnp.where((lane1 >= 3) & (lane1 < 6), 1.0, 0.0)
    ones_q = jnp.where(lane1 < 3, 1.0, 0.0)
    extra = lax.broadcasted_iota(I32, (ATTN_V_ROWS - HEAD_DIM, tq), 0)
    ones_row = jnp.where(extra == 0, 1.0, 0.0).astype(BF16)

    def fill(i, carry):
        st = pl.multiple_of(i * tq, tq)
        pieces = _bf16_pieces(c_ref[0, pl.ds(st, tq), :] * LOG2E)
        for hh in heads:
            kx_sc[hh, pl.ds(st, tq), :] = (spread(pieces, hh, 0, -1.0) + ones_k).astype(BF16)
            qx_sc[hh, pl.ds(st, tq), :] = (spread(pieces, hh, 3, 1.0) + ones_q).astype(BF16)
            v = v_ref[0, pl.ds(st, tq), head_lanes(hh)]
            vt_sc[hh, 0:HEAD_DIM, pl.ds(st, tq)] = v.astype(F32).T.astype(BF16)
            vt_sc[hh, HEAD_DIM:, pl.ds(st, tq)] = ones_row
        return carry

    lax.fori_loop(0, n_t, fill, 0)

    def reset():
        m_sc[...] = jnp.full(m_sc.shape, -jnp.inf, F32)
        acc_sc[...] = jnp.zeros(acc_sc.shape, F32)

    def scores_into(i, j, slot):
        sq = pl.multiple_of(i * tq, tq)
        sk = pl.multiple_of(j * tq, tq)
        for hh in heads:
            q_aug = jnp.concatenate(
                [q_ref[0, pl.ds(sq, tq), head_lanes(hh)], qx_sc[hh, pl.ds(sq, tq), :]], axis=-1)
            k_aug = jnp.concatenate(
                [k_ref[0, pl.ds(sk, tq), head_lanes(hh)], kx_sc[hh, pl.ds(sk, tq), :]], axis=-1)
            s_sc[slot, hh] = lax.dot_general(k_aug, q_aug, (((1,), (1,)), ((), ())),
                                             preferred_element_type=F32)

    def update(j, slot, masked):
        st = pl.multiple_of(j * tq, tq)
        for hh in heads:
            s = s_sc[slot, hh]
            if masked:
                key = lax.broadcasted_iota(I32, (tq, tq), 0)
                qry = lax.broadcasted_iota(I32, (tq, tq), 1)
                s = jnp.where(key <= qry, s, -jnp.inf)
            m_prev = m_sc[hh]
            m_new = jnp.maximum(m_prev, jnp.max(s, axis=0, keepdims=True))
            alpha = jnp.exp2(m_prev - m_new)
            p = jnp.exp2(s - m_new)
            pv = jnp.dot(vt_sc[hh, :, pl.ds(st, tq)], p.astype(BF16), preferred_element_type=F32)
            acc_sc[hh] = alpha * acc_sc[hh] + pv
            m_sc[hh] = m_new

    def finalize(i):
        sq = pl.multiple_of(i * tq, tq)
        for hh in heads:
            acc = acc_sc[hh]
            o_ref[0, pl.ds(sq, tq), head_lanes(hh)] = (
                acc[0:HEAD_DIM] / acc[HEAD_DIM:HEAD_DIM + 1]).T.astype(BF16)
        reset()

    def tile(i, j, slot):
        diag = j == i
        ni = jnp.where(diag, i + 1, i)
        nj = jnp.where(diag, 0, j + 1)
        si = jnp.minimum(ni, n_t - 1)
        sj = jnp.where(ni == n_t, n_t - 1, nj)

        @pl.when(diag)
        def _():
            scores_into(si, sj, 1 - slot)
            update(j, slot, True)
            finalize(i)

        @pl.when(jnp.logical_not(diag))
        def _():
            scores_into(si, sj, 1 - slot)
            update(j, slot, False)

        return ni, nj

    reset()
    scores_into(0, 0, 0)

    def body(t, ij):
        ij = tile(ij[0], ij[1], 0)
        return tile(ij[0], ij[1], 1)

    n_tiles = n_t * (n_t + 1) // 2
    assert n_tiles % 2 == 0
    lax.fori_loop(0, n_tiles // 2, body, (jnp.int32(0), jnp.int32(0)))


def _fox_attention(q, k, v, cum):
    b, s, _ = q.shape
    tq, hp = TQ, HEADS_PER_STEP
    head = pl.BlockSpec((1, s, hp * HEAD_DIM), lambda bi, g: (bi, 0, g))
    return pl.pallas_call(
        _attn_kernel,
        grid=(b, ATTN_HEADS // hp),
        in_specs=[head, head, head, pl.BlockSpec((1, s, LANES), lambda bi, g: (bi, 0, 0))],
        out_specs=head,
        out_shape=jax.ShapeDtypeStruct((b, s, ATTN_WIDTH), BF16),
        scratch_shapes=[pltpu.VMEM((hp, s, LANES), BF16),
                        pltpu.VMEM((hp, s, LANES), BF16),
                        pltpu.VMEM((hp, ATTN_V_ROWS, s), BF16),
                        pltpu.VMEM((2, hp, tq, tq), F32),
                        pltpu.VMEM((hp, 1, tq), F32),
                        pltpu.VMEM((hp, ATTN_V_ROWS, tq), F32)],
        compiler_params=_params(2),
        name="fox_attention",
    )(q, k, v, cum)


def _conv_kernel(u_ref, w_ref, cb_ref, lg_ref, lb_ref, o_ref, ubuf, shifted, wb):
    i = pl.program_id(1)
    ts, halo, ck = TS_CONV, CONV_HALO, CONV_CHUNK

    @pl.when(i == 0)
    def _():
        ubuf[0:halo, :] = jnp.zeros((halo, CONV_WIDTH), F32)
        for j in range(CONV_KERNEL):
            wb[j * SUBLANES:(j + 1) * SUBLANES, :] = jnp.broadcast_to(
                w_ref[j:j + 1, :], (SUBLANES, CONV_WIDTH))

    ubuf[halo:halo + ts, :] = u_ref[0].astype(F32)
    base = halo - (CONV_KERNEL - 1)
    span = shifted.shape[1]
    for r in range(1, SUBLANES):
        shifted[r] = ubuf[r:r + span, :]
    for c in range(ts // ck):
        acc = jnp.zeros((ck, CONV_WIDTH), F32)
        for j in range(CONV_KERNEL):
            off = c * ck + base + j
            r, al = off % SUBLANES, off - off % SUBLANES
            slab = ubuf[al:al + ck, :] if r == 0 else shifted[r, al:al + ck, :]
            wj = wb[j * SUBLANES:(j + 1) * SUBLANES, :]
            acc = acc + slab * jnp.concatenate([wj] * (ck // SUBLANES), axis=0)
        y = acc + cb_ref[...]
        mu = jnp.mean(y, axis=-1, keepdims=True)
        yc = y - mu
        var = jnp.mean(yc * yc, axis=-1, keepdims=True)
        z = yc * lax.rsqrt(var + EPS) * lg_ref[...] + lb_ref[...]
        o_ref[0, c * ck:(c + 1) * ck, :] = (z * jax.nn.sigmoid(z)).astype(BF16)
    ubuf[0:halo, :] = ubuf[ts:ts + halo, :]


def _conv_module(u, conv_w, conv_b, ln_g, ln_b):
    b, s, cw = u.shape
    ts = TS_CONV
    return pl.pallas_call(
        _conv_kernel,
        grid=(b, s // ts),
        in_specs=[pl.BlockSpec((1, ts, cw), lambda bi, i: (bi, i, 0)),
                  _resident((CONV_KERNEL, cw)), _resident((1, cw)),
                  _resident((1, cw)), _resident((1, cw))],
        out_specs=pl.BlockSpec((1, ts, cw), lambda bi, i: (bi, i, 0)),
        out_shape=jax.ShapeDtypeStruct((b, s, cw), BF16),
        scratch_shapes=[pltpu.VMEM((CONV_HALO + ts, cw), F32),
                        pltpu.VMEM((SUBLANES, ts + CONV_HALO - SUBLANES, cw), F32),
                        pltpu.VMEM((CONV_KERNEL * SUBLANES, cw), F32)],
        compiler_params=_params(2),
        name="conv_module",
    )(u, conv_w, conv_b, ln_g, ln_b)


def _pack_rows(val):
    lo = lax.bitcast_convert_type(val[:, :HALF].astype(BF16).astype(F32), U32)
    hi = lax.bitcast_convert_type(val[:, HALF:].astype(BF16).astype(F32), U32)
    return hi | (lo >> 16)


def _unpack_rows(word):
    lo = lax.bitcast_convert_type(word << 16, F32)
    hi = lax.bitcast_convert_type(word & jnp.uint32(0xFFFF0000), F32)
    return lo, hi


def _store_row_slabs(ref, word):
    rows = word.shape[0]
    for s in range(ROW_SLAB):
        ref[pl.ds(s, rows, stride=ROW_SLAB), :] = word[:, s * LANES:(s + 1) * LANES]


def _load_row_slabs(ref, start, rows):
    parts = [ref[pl.ds(start * ROW_SLAB + s, rows, stride=ROW_SLAB), :] for s in range(ROW_SLAB)]
    return jnp.concatenate(parts, axis=-1)


def _outproj_kernel(at_ref, cv_ref, x_ref, wo_ref, g1_ref, a2_ref, s2_ref, wrh_ref, wrl_ref, br_ref,
                    x1_ref, z_ref, lg_ref):
    mix = (jnp.dot(at_ref[0], wo_ref[0:ATTN_WIDTH, :], preferred_element_type=F32)
           + jnp.dot(cv_ref[0], wo_ref[ATTN_WIDTH:, :], preferred_element_type=F32))
    x1 = x_ref[0] + g1_ref[0] * mix
    x1_ref[0] = x1
    h2 = x1 * lax.rsqrt(jnp.mean(x1 * x1, axis=-1, keepdims=True) + EPS) * a2_ref[0] + s2_ref[0]
    _store_row_slabs(z_ref, _pack_rows(h2))
    hi = h2.astype(BF16)
    lo = (h2 - hi.astype(F32)).astype(BF16)
    lg_ref[0] = (jnp.dot(hi, wrh_ref[...], preferred_element_type=F32)
                 + jnp.dot(lo, wrh_ref[...], preferred_element_type=F32)
                 + jnp.dot(hi, wrl_ref[...], preferred_element_type=F32)) + br_ref[...]


def _out_proj(attn, conv, x, w_out, g1, a2, s2, wr_hi, wr_lo, b_r):
    b, s, d = x.shape
    tm = TM_PROJ
    nt = s // tm
    tok = lambda w: pl.BlockSpec((1, tm, w), lambda bi, i: (bi, i, 0))
    per_batch = pl.BlockSpec((1, 1, d), lambda bi, i: (bi, 0, 0))
    return pl.pallas_call(
        _outproj_kernel,
        grid=(b, nt),
        in_specs=[tok(ATTN_WIDTH), tok(CONV_WIDTH), tok(d), _resident((d, d)),
                  per_batch, per_batch, per_batch,
                  _resident((d, LANES)), _resident((d, LANES)), _resident((1, LANES))],
        out_specs=[tok(d),
                   pl.BlockSpec((tm * ROW_SLAB, LANES), lambda bi, i: (bi * nt + i, 0)),
                   tok(LANES)],
        out_shape=[jax.ShapeDtypeStruct((b, s, d), F32),
                   jax.ShapeDtypeStruct((b * s * ROW_SLAB, LANES), U32),
                   jax.ShapeDtypeStruct((b, s, LANES), F32)],
        compiler_params=_params(2),
        name="out_proj",
    )(attn, conv, x, w_out, g1, a2, s2, wr_hi, wr_lo, b_r)


def _route_kernel(lg_ref, ri_ref, rw_ref, cnt_ref, carry):
    i = pl.program_id(0)
    tm = TM_ROUTE

    @pl.when(i == 0)
    def _():
        carry[...] = jnp.zeros(carry.shape, F32)

    l = lg_ref[...]
    lane = lax.broadcasted_iota(I32, (tm, LANES), 1).astype(F32)
    vals, idxs = [], []
    for _ in range(TOP_K):
        m = jnp.max(l, axis=-1, keepdims=True)
        ix = jnp.min(jnp.where(l == m, lane, float(LANES)), axis=-1, keepdims=True)
        vals.append(m)
        idxs.append(ix)
        l = jnp.where(lane == ix, -jnp.inf, l)
    es = [jnp.exp(v - vals[0]) for v in vals]
    den = es[0] + es[1] + es[2] + es[3]
    onehot = jnp.zeros((tm, LANES), F32)
    for ix in idxs:
        onehot = onehot + jnp.where(lane == ix, 1.0, 0.0)
    row = lax.broadcasted_iota(I32, (tm, tm), 0)
    col = lax.broadcasted_iota(I32, (tm, tm), 1)
    strict = (col < row).astype(BF16)
    before = jnp.dot(strict, onehot.astype(BF16), preferred_element_type=F32) + carry[0:1, :]
    ri = jnp.zeros((tm, LANES), F32)
    rw = jnp.zeros((tm, LANES), F32)
    for k in range(TOP_K):
        rank = jnp.sum(jnp.where(lane == idxs[k], before, 0.0), axis=-1, keepdims=True)
        ri = jnp.where(lane == k, idxs[k], ri)
        ri = jnp.where(lane == TOP_K + k, rank, ri)
        rw = jnp.where(lane == k, es[k] / den, rw)
    ri_ref[...] = ri.astype(I32)
    rw_ref[...] = rw
    carry[0:1, :] = carry[0:1, :] + jnp.sum(onehot, axis=0, keepdims=True)
    cnt_ref[...] = carry[...]


def _route(logits):
    t = logits.shape[0]
    tm = TM_ROUTE
    return pl.pallas_call(
        _route_kernel,
        grid=(t // tm,),
        in_specs=[pl.BlockSpec((tm, LANES), lambda i: (i, 0))],
        out_specs=[pl.BlockSpec((tm, LANES), lambda i: (i, 0)),
                   pl.BlockSpec((tm, LANES), lambda i: (i, 0)),
                   pl.BlockSpec((SUBLANES, LANES), lambda i: (0, 0))],
        out_shape=[jax.ShapeDtypeStruct((t, LANES), I32),
                   jax.ShapeDtypeStruct((t, LANES), F32),
                   jax.ShapeDtypeStruct((SUBLANES, LANES), F32)],
        scratch_shapes=[pltpu.VMEM((SUBLANES, LANES), F32)],
        compiler_params=_params(1),
        name="route",
    )(logits)


def _issue_row_gather(idx_ref, idx_base, n_rows, src_hbm, dst_buf, sem):
    def body(g, carry):
        for u in range(GATHER_UNROLL):
            r = g * GATHER_UNROLL + u
            t = idx_ref[idx_base + r]
            pltpu.make_async_copy(
                src_hbm.at[pl.ds(pl.multiple_of(t * ROW_SLAB, ROW_SLAB), ROW_SLAB), :],
                dst_buf.at[pl.ds(pl.multiple_of(r * ROW_SLAB, ROW_SLAB), ROW_SLAB), :],
                sem).start(priority=u % 2)
        return carry
    lax.fori_loop(0, n_rows // GATHER_UNROLL, body, 0)


def _wait_row_gather(n_rows, src_hbm, dst_buf, sem):
    pltpu.make_async_copy(src_hbm.at[pl.ds(0, n_rows * ROW_SLAB), :], dst_buf, sem).wait()


def _dispatch_kernel(dest_ref, pad_ref, z_ref, xs_hbm, stage, zeros, sem, zsem):
    i = pl.program_id(0)
    n = pl.num_programs(0)
    tm = TM_DISP
    pad_rows = TM_MOE * ROW_SLAB

    def zero_copy(slot, n_slots):
        start = pl.multiple_of(slot * ROW_SLAB, ROW_SLAB)
        return pltpu.make_async_copy(zeros.at[pl.ds(0, n_slots * ROW_SLAB), :],
                                     xs_hbm.at[pl.ds(start, n_slots * ROW_SLAB), :], zsem)

    def for_pad_pieces(e, fn):
        slot = pad_ref[e]
        n = pad_ref[N_EXPERTS + e]
        size = TM_MOE // 2
        while size >= 1:
            piece = zero_copy(slot, size)
            pl.when((n & size) != 0)(functools.partial(fn, piece))
            slot = slot + (n & size)
            size //= 2

    @pl.when(i == 0)
    def _():
        zeros[...] = jnp.zeros(zeros.shape, U32)
        first_free = pad_ref[2 * N_EXPERTS]
        n_total = xs_hbm.shape[0] // pad_rows

        def start_all(e, carry):
            for_pad_pieces(e, lambda piece: piece.start())
            return carry

        def wait_all(e, carry):
            for_pad_pieces(e, lambda piece: piece.wait())
            return carry

        def start_free(b, carry):
            zero_copy(b * TM_MOE, TM_MOE).start()
            return carry

        def wait_free(b, carry):
            zero_copy(b * TM_MOE, TM_MOE).wait()
            return carry

        lax.fori_loop(0, N_EXPERTS, start_all, 0)
        lax.fori_loop(first_free, n_total, start_free, 0)
        lax.fori_loop(0, N_EXPERTS, wait_all, 0)
        lax.fori_loop(first_free, n_total, wait_free, 0)

    cur = i % 2
    stage[cur] = z_ref[...]

    def body(g, carry):
        for u in range(GATHER_UNROLL // TOP_K):
            r = g * (GATHER_UNROLL // TOP_K) + u
            src = stage.at[cur, pl.ds(pl.multiple_of(r * ROW_SLAB, ROW_SLAB), ROW_SLAB), :]
            for k in range(TOP_K):
                d = dest_ref[(i * tm + r) * TOP_K + k]
                pltpu.make_async_copy(
                    src, xs_hbm.at[pl.ds(pl.multiple_of(d * ROW_SLAB, ROW_SLAB), ROW_SLAB), :],
                    sem.at[cur]).start(priority=k % 2)
        return carry

    lax.fori_loop(0, tm * TOP_K // GATHER_UNROLL, body, 0)

    def drain(slot):
        for _ in range(TOP_K):
            pltpu.make_async_copy(stage.at[slot], xs_hbm.at[pl.ds(0, tm * ROW_SLAB), :], sem.at[slot]).wait()

    @pl.when(i > 0)
    def _():
        drain(1 - cur)

    @pl.when(i == n - 1)
    def _():
        drain(cur)


def _dispatch(dest_flat, pad_start, z, n_slots):
    t = z.shape[0] // ROW_SLAB
    tm = TM_DISP
    return pl.pallas_call(
        _dispatch_kernel,
        grid_spec=pltpu.PrefetchScalarGridSpec(
            num_scalar_prefetch=2,
            grid=(t // tm,),
            in_specs=[pl.BlockSpec((tm * ROW_SLAB, LANES), lambda i, dr, pr: (i, 0))],
            out_specs=pl.BlockSpec(memory_space=pl.ANY),
            scratch_shapes=[pltpu.VMEM((2, tm * ROW_SLAB, LANES), U32),
                            pltpu.VMEM((TM_MOE * ROW_SLAB, LANES), U32),
                            pltpu.SemaphoreType.DMA((2,)), pltpu.SemaphoreType.DMA(())]),
        out_shape=jax.ShapeDtypeStruct((n_slots * ROW_SLAB, LANES), U32),
        compiler_params=_params(1),
        name="dispatch",
    )(dest_flat, pad_start, z)


SCHED_W = 5
S_EXPERT, S_SLOT, S_NEXT, S_C0, S_C1 = range(SCHED_W)


def _expert_weights_step(sched_ref, b, n_used, w_hbm, wbuf, stage, sems, compute):
    n_mat = len(w_hbm)
    n_chunk = wbuf.shape[2] // W_CHUNK

    def rows(rc):
        return pl.ds(pl.multiple_of(rc * W_CHUNK, W_CHUNK), W_CHUNK)

    def copies(e, rc, j):
        return [pltpu.make_async_copy(w_hbm[m].at[e, rows(rc), :], stage.at[j, m], sems.at[j, m])
                for m in range(n_mat)]

    def convert(slot, rc, j):
        for m in range(n_mat):
            wbuf[slot, m, rows(rc), :] = stage[j, m].astype(BF16)

    def load_now(e, slot, rc0, rc1):
        def body(rc, carry):
            for cp in copies(e, rc, 0):
                cp.start()
            for cp in copies(e, rc, 0):
                cp.wait()
            convert(slot, rc, 0)
            return carry
        lax.fori_loop(rc0, rc1, body, 0)

    @pl.when(b == 0)
    def _():
        load_now(sched_ref[S_EXPERT], 0, 0, n_chunk)

    @pl.when(b < n_used)
    def _():
        base = b * SCHED_W
        slot, nxt = sched_ref[base + S_SLOT], sched_ref[base + S_NEXT]
        c0, c1 = sched_ref[base + S_C0], sched_ref[base + S_C1]
        def start_chunk(c, carry):
            for cp in copies(nxt, c, c - c0):
                cp.start()
            return carry
        lax.fori_loop(c0, jnp.minimum(c0 + N_STAGE, c1), start_chunk, 0)
        compute(slot)
        for j in range(N_STAGE):
            @pl.when(c0 + j < c1)
            def _():
                for cp in copies(nxt, c0 + j, j):
                    cp.wait()
                convert(1 - slot, c0 + j, j)
        load_now(nxt, 1 - slot, jnp.minimum(c0 + N_STAGE, c1), c1)


def _moe_up_kernel(sched_ref, nu_ref, xs_ref, wg_hbm, bg_ref, wu_hbm, bu_ref, hid_ref, wbuf, stage, sems):
    b = pl.program_id(0)
    tm = TM_MOE

    def compute(slot):
        lo, hi = _unpack_rows(_load_row_slabs(xs_ref, 0, tm))
        x = jnp.concatenate([lo.astype(BF16), hi.astype(BF16)], axis=-1)
        g = jnp.dot(x, wbuf[slot, 0], preferred_element_type=F32) + bg_ref[0]
        u = jnp.dot(x, wbuf[slot, 1], preferred_element_type=F32) + bu_ref[0]
        g = jnp.minimum(g, SWIGLU_LIMIT)
        u = jnp.clip(u, -SWIGLU_LIMIT, SWIGLU_LIMIT)
        hid_ref[...] = ((u + 1.0) * (g * jax.nn.sigmoid(SWIGLU_ALPHA * g))).astype(BF16)

    _expert_weights_step(sched_ref, b, nu_ref[0], (wg_hbm, wu_hbm), wbuf, stage, sems, compute)

    @pl.when(b >= nu_ref[0])
    def _():
        hid_ref[...] = jnp.zeros(hid_ref.shape, BF16)


def _expert_scratch(n_mat, k, n):
    return [pltpu.VMEM((2, n_mat, k, n), BF16),
            pltpu.VMEM((N_STAGE, n_mat, W_CHUNK, n), F32),
            pltpu.SemaphoreType.DMA((N_STAGE, n_mat))]


def _moe_up(sched, n_used, xs, w_gate, b_gate, w_up, b_up):
    e, d, f = w_gate.shape
    n_blocks = sched.shape[0] // SCHED_W
    tm = TM_MOE
    hbm = pl.BlockSpec(memory_space=pl.ANY)
    bspec = pl.BlockSpec((1, 1, f), lambda b, sc, nu: (sc[b * SCHED_W + S_EXPERT], 0, 0))
    return pl.pallas_call(
        _moe_up_kernel,
        grid_spec=pltpu.PrefetchScalarGridSpec(
            num_scalar_prefetch=2,
            grid=(n_blocks,),
            in_specs=[pl.BlockSpec((tm * ROW_SLAB, LANES), lambda b, sc, nu: (jnp.minimum(b, nu[0] - 1), 0)),
                      hbm, bspec, hbm, bspec],
            out_specs=pl.BlockSpec((tm, f), lambda b, sc, nu: (b, 0)),
            scratch_shapes=_expert_scratch(2, d, f)),
        out_shape=jax.ShapeDtypeStruct((n_blocks * tm, f), BF16),
        compiler_params=_params(1),
        name="moe_up",
    )(sched, n_used, xs, w_gate, b_gate, w_up, b_up)


def _moe_down_kernel(sched_ref, nu_ref, hid_ref, wd_hbm, bd_ref, ys_ref, wbuf, stage, sems):
    b = pl.program_id(0)

    def compute(slot):
        out = jnp.dot(hid_ref[...], wbuf[slot, 0], preferred_element_type=F32) + bd_ref[0]
        _store_row_slabs(ys_ref, _pack_rows(out))

    _expert_weights_step(sched_ref, b, nu_ref[0], (wd_hbm,), wbuf, stage, sems, compute)

    @pl.when(b >= nu_ref[0])
    def _():
        ys_ref[...] = jnp.zeros(ys_ref.shape, U32)


def _moe_down(sched, n_used, hid, w_down, b_down):
    e, f, d = w_down.shape
    n_blocks = sched.shape[0] // SCHED_W
    tm = TM_MOE
    return pl.pallas_call(
        _moe_down_kernel,
        grid_spec=pltpu.PrefetchScalarGridSpec(
            num_scalar_prefetch=2,
            grid=(n_blocks,),
            in_specs=[pl.BlockSpec((tm, f), lambda b, sc, nu: (b, 0)),
                      pl.BlockSpec(memory_space=pl.ANY),
                      pl.BlockSpec((1, 1, d), lambda b, sc, nu: (sc[b * SCHED_W + S_EXPERT], 0, 0))],
            out_specs=pl.BlockSpec((tm * ROW_SLAB, LANES), lambda b, sc, nu: (b, 0)),
            scratch_shapes=_expert_scratch(1, f, d)),
        out_shape=jax.ShapeDtypeStruct((n_blocks * tm * ROW_SLAB, LANES), U32),
        compiler_params=_params(1),
        name="moe_down",
    )(sched, n_used, hid, w_down, b_down)


def _combine_kernel(dest_ref, ys_hbm, x1_ref, rw_ref, g2_ref, o_ref, buf, sem):
    i = pl.program_id(0)
    n = pl.num_programs(0)
    tm = TM_COMB
    rows = TOP_K * tm

    @pl.when(i == 0)
    def _():
        _issue_row_gather(dest_ref, 0, rows, ys_hbm, buf.at[0], sem.at[0])

    @pl.when(i + 1 < n)
    def _():
        nxt = (i + 1) % 2
        _issue_row_gather(dest_ref, (i + 1) * rows, rows, ys_hbm, buf.at[nxt], sem.at[nxt])

    cur = i % 2
    _wait_row_gather(rows, ys_hbm, buf.at[cur], sem.at[cur])
    rw = rw_ref[...]
    y_lo = jnp.zeros((tm, HALF), F32)
    y_hi = jnp.zeros((tm, HALF), F32)
    for k in range(TOP_K):
        lo, hi = _unpack_rows(_load_row_slabs(buf.at[cur], k * tm, tm))
        y_lo = y_lo + rw[:, k:k + 1] * lo
        y_hi = y_hi + rw[:, k:k + 1] * hi
    o_ref[...] = x1_ref[...] + g2_ref[0] * jnp.concatenate([y_lo, y_hi], axis=-1)


def _combine(dest_km, ys, x1, rw, g2, seq):
    t, d = x1.shape
    tm = TM_COMB
    per_seq = seq // tm
    return pl.pallas_call(
        _combine_kernel,
        grid_spec=pltpu.PrefetchScalarGridSpec(
            num_scalar_prefetch=1,
            grid=(t // tm,),
            in_specs=[pl.BlockSpec(memory_space=pl.ANY),
                      pl.BlockSpec((tm, d), lambda i, dr: (i, 0)),
                      pl.BlockSpec((tm, LANES), lambda i, dr: (i, 0)),
                      pl.BlockSpec((1, 1, d), lambda i, dr: (i // per_seq, 0, 0))],
            out_specs=pl.BlockSpec((tm, d), lambda i, dr: (i, 0)),
            scratch_shapes=[pltpu.VMEM((2, TOP_K * tm * ROW_SLAB, LANES), U32),
                            pltpu.SemaphoreType.DMA((2,))]),
        out_shape=jax.ShapeDtypeStruct((t, d), F32),
        compiler_params=_params(1),
        name="combine",
    )(dest_km, ys, x1, rw, g2)


def _pad_cols(w, n):
    return jnp.pad(w, ((0, 0), (0, n - w.shape[1])))


def _layer(x, mod, norm_mix_g, norm_ffn_g, w_in, b_f, q_norm_g, k_norm_g, conv_w, conv_b,
           conv_ln_g, conv_ln_b, w_out, w_router, b_router, w_gate, b_gate, w_up, b_up,
           w_down, b_down):
    b, s, d = x.shape
    t = b * s
    shift1, scale1, gate1, shift2, scale2, gate2 = [m[:, None, :] for m in jnp.split(mod, 6, axis=-1)]
    a1 = norm_mix_g[None, None, :] * (1.0 + scale1)
    a2 = norm_ffn_g[None, None, :] * (1.0 + scale2)

    aw = ATTN_WIDTH
    w_cat = jnp.concatenate([w_in[:, :3 * aw], w_in[:, 3 * aw + ATTN_HEADS:]], axis=1).astype(BF16)
    w_f = _pad_cols(w_in[:, 3 * aw:3 * aw + ATTN_HEADS], LANES).astype(BF16)
    b_f_pad = _pad_cols(b_f[None, :], LANES)

    q, k, v, u, logf = _in_proj(x, a1, shift1, w_cat, w_f, b_f_pad,
                                q_norm_g[None, :], k_norm_g[None, :])
    attn = _fox_attention(q, k, v, _forget_cumsum(logf))
    conv = _conv_module(u, conv_w, conv_b[None, :], conv_ln_g[None, :], conv_ln_b[None, :])

    wr = _pad_cols(w_router, LANES)
    wr_hi = wr.astype(BF16)
    wr_lo = (wr - wr_hi.astype(F32)).astype(BF16)
    b_r = jnp.concatenate([b_router, jnp.full((LANES - N_EXPERTS,), NEG_BIG, F32)])[None, :]
    x1, z, logits = _out_proj(attn, conv, x, w_out.astype(BF16), gate1, a2, shift2, wr_hi, wr_lo, b_r)

    ri, rw, cnt = _route(logits.reshape(t, LANES))

    tm = TM_MOE
    n_blocks = t * TOP_K // tm + N_EXPERTS
    idx = ri[:, 0:TOP_K]
    rank = ri[:, TOP_K:2 * TOP_K]
    counts = cnt[0, :N_EXPERTS].astype(I32)
    padded = (counts + tm - 1) // tm * tm
    padded_end = jnp.cumsum(padded)
    padded_start = padded_end - padded
    dest = padded_start[idx] + rank
    block_start = jnp.arange(n_blocks, dtype=I32)[:, None] * tm
    block_e = jnp.minimum(jnp.sum((padded_end[None, :] <= block_start).astype(I32), axis=1), N_EXPERTS - 1)
    n_used = (padded_end[-1:] // tm).astype(I32)
    dest_km = dest.reshape(t // TM_COMB, TM_COMB, TOP_K).transpose(0, 2, 1).reshape(-1)

    pad_table = jnp.concatenate([padded_start + counts, padded - counts, n_used])
    xs = _dispatch(dest.reshape(-1), pad_table, z, n_blocks * tm)

    experts = jnp.arange(N_EXPERTS, dtype=I32)
    has = counts > 0
    later = (experts[None, :] > experts[:, None]) & has[None, :]
    next_e = jnp.min(jnp.where(later, experts[None, :], N_EXPERTS), axis=1)
    run_slot = (jnp.cumsum(has.astype(I32)) - 1) % 2
    of_block = block_e[:, None] == experts[None, :]

    def per_block(table):
        return jnp.sum(jnp.where(of_block, table[None, :], 0), axis=1)

    run_len = per_block(jnp.maximum(padded // tm, 1))
    pos = jnp.arange(n_blocks, dtype=I32) - per_block(padded_start // tm)
    nxt = per_block(next_e)
    live = nxt < N_EXPERTS
    c0 = jnp.where(live, N_WCHUNK * pos // run_len, 0)
    c1 = jnp.where(live, N_WCHUNK * (pos + 1) // run_len, 0)
    sched = jnp.stack([block_e, per_block(run_slot), jnp.where(live, nxt, 0), c0, c1], axis=1).reshape(-1)

    hid = _moe_up(sched, n_used, xs, w_gate, b_gate[:, None, :], w_up, b_up[:, None, :])
    ys = _moe_down(sched, n_used, hid, w_down, b_down[:, None, :])
    out = _combine(dest_km, ys, x1.reshape(t, d), rw, gate2, s)
    return out.reshape(b, s, d)


def kernel(x, c, ada_w, ada_b, norm_mix_g, norm_ffn_g, w_in, b_f, q_norm_g, k_norm_g, conv_w, conv_b,
           conv_ln_g, conv_ln_b, w_out, w_router, b_router, w_gate, b_gate, w_up, b_up, w_down, b_down):
    b = x.shape[0]
    c_pad = jnp.pad(c, ((0, SUBLANES - b), (0, 0)))
    for l in range(ada_w.shape[0]):
        mod = _ada_mod(c_pad, ada_w[l], ada_b[l])[:b]
        x = _layer(x, mod, norm_mix_g[l], norm_ffn_g[l], w_in[l], b_f[l], q_norm_g[l], k_norm_g[l],
                   conv_w[l], conv_b[l], conv_ln_g[l], conv_ln_b[l], w_out[l], w_router[l],
                   b_router[l], w_gate[l], b_gate[l], w_up[l], b_up[l], w_down[l], b_down[l])
    return x
```

```python
import functools

import jax
import jax.numpy as jnp
from jax import lax
from jax.experimental import pallas as pl
from jax.experimental.pallas import tpu as pltpu

F32 = jnp.float32
BF16 = jnp.bfloat16
I32 = jnp.int32
U32 = jnp.uint32

D_MODEL = 2048
ATTN_HEADS = 8
HEAD_DIM = 128
ATTN_WIDTH = ATTN_HEADS * HEAD_DIM
CONV_WIDTH = D_MODEL - ATTN_WIDTH
CONV_KERNEL = 31
N_EXPERTS = 32
TOP_K = 4
SWIGLU_LIMIT = 7.0
SWIGLU_ALPHA = 1.702
EPS = 1e-6
LOG2E = 1.4426950408889634

LANES = 128
SUBLANES = 8
HALF = D_MODEL // 2
ROW_SLAB = HALF // LANES
VMEM_LIMIT = 56 * 1024 * 1024

TM_PROJ = 512
TQ = 512
ATTN_V_ROWS = HEAD_DIM + 16
HEADS_PER_STEP = 2
CUM_CHUNK = 256
TS_CONV = 256
CONV_HALO = 32
CONV_CHUNK = 32
TM_ROUTE = 512
TM_MOE = 256
TM_DISP = 256
TM_COMB = 256
GATHER_UNROLL = 32
W_CHUNK = 256
N_WCHUNK = D_MODEL // W_CHUNK
N_STAGE = 2
NEG_BIG = -1e30


def _params(n_axes):
    return pltpu.CompilerParams(
        dimension_semantics=("arbitrary",) * n_axes, vmem_limit_bytes=VMEM_LIMIT)


def _resident(shape):
    nd = len(shape)
    return pl.BlockSpec(shape, lambda *_: (0,) * nd, pipeline_mode=pl.Buffered(1))


def _ada_kernel(c_ref, w_ref, b_ref, o_ref):
    c = c_ref[...]
    c_act = (c * jax.nn.sigmoid(c)).astype(BF16)
    o_ref[...] = jnp.dot(c_act, w_ref[...].astype(BF16), preferred_element_type=F32) + b_ref[...]


def _ada_mod(c_pad, ada_w, ada_b):
    rows, d = c_pad.shape
    n = ada_w.shape[1]
    tn = 1024
    return pl.pallas_call(
        _ada_kernel,
        grid=(n // tn,),
        in_specs=[pl.BlockSpec((rows, d), lambda j: (0, 0)),
                  pl.BlockSpec((d, tn), lambda j: (0, j)),
                  pl.BlockSpec((1, tn), lambda j: (0, j))],
        out_specs=pl.BlockSpec((rows, tn), lambda j: (0, j)),
        out_shape=jax.ShapeDtypeStruct((rows, n), F32),
        compiler_params=_params(1),
        name="ada_mod",
    )(c_pad, ada_w, ada_b.reshape(1, n))


def _log_sigmoid(x):
    return jnp.minimum(x, 0.0) - jnp.log1p(jnp.exp(-jnp.abs(x)))


def _head_rms(y, g):
    outs = []
    for h in range(ATTN_HEADS):
        yh = y[:, h * HEAD_DIM:(h + 1) * HEAD_DIM]
        r = lax.rsqrt(jnp.mean(yh * yh, axis=-1, keepdims=True) + EPS)
        outs.append(yh * r * g)
    return jnp.concatenate(outs, axis=-1)


def _inproj_kernel(x_ref, a_ref, s_ref, w_ref, wf_ref, bf_ref, qg_ref, kg_ref,
                   q_ref, k_ref, v_ref, u_ref, f_ref):
    x = x_ref[0]
    h = x * lax.rsqrt(jnp.mean(x * x, axis=-1, keepdims=True) + EPS) * a_ref[0] + s_ref[0]
    hb = h.astype(BF16)
    aw = ATTN_WIDTH
    q = jnp.dot(hb, w_ref[:, 0:aw], preferred_element_type=F32)
    q_ref[0] = (_head_rms(q, qg_ref[...]) * (LOG2E * HEAD_DIM ** -0.5)).astype(BF16)
    k = jnp.dot(hb, w_ref[:, aw:2 * aw], preferred_element_type=F32)
    k_ref[0] = _head_rms(k, kg_ref[...]).astype(BF16)
    v_ref[0] = jnp.dot(hb, w_ref[:, 2 * aw:3 * aw], preferred_element_type=F32).astype(BF16)
    a = jnp.dot(hb, w_ref[:, 3 * aw:3 * aw + CONV_WIDTH], preferred_element_type=F32)
    g = jnp.dot(hb, w_ref[:, 3 * aw + CONV_WIDTH:], preferred_element_type=F32)
    u_ref[0] = (a * jax.nn.sigmoid(g)).astype(BF16)
    fl = jnp.dot(hb, wf_ref[...], preferred_element_type=F32) + bf_ref[...]
    f_ref[0] = _log_sigmoid(fl)


def _in_proj(x, a1, s1, w_cat, w_f, b_f, q_g, k_g):
    b, s, d = x.shape
    tm = TM_PROJ
    ncat = w_cat.shape[1]
    tok = lambda w: pl.BlockSpec((1, tm, w), lambda bi, i: (bi, i, 0))
    per_batch = pl.BlockSpec((1, 1, d), lambda bi, i: (bi, 0, 0))
    return pl.pallas_call(
        _inproj_kernel,
        grid=(b, s // tm),
        in_specs=[tok(d), per_batch, per_batch,
                  _resident((d, ncat)), _resident((d, LANES)), _resident((1, LANES)),
                  _resident((1, HEAD_DIM)), _resident((1, HEAD_DIM))],
        out_specs=[tok(ATTN_WIDTH), tok(ATTN_WIDTH), tok(ATTN_WIDTH), tok(CONV_WIDTH), tok(LANES)],
        out_shape=[jax.ShapeDtypeStruct((b, s, ATTN_WIDTH), BF16)] * 3
        + [jax.ShapeDtypeStruct((b, s, CONV_WIDTH), BF16),
           jax.ShapeDtypeStruct((b, s, LANES), F32)],
        compiler_params=_params(2),
        name="in_proj",
    )(x, a1, s1, w_cat, w_f, b_f, q_g, k_g)


def _bf16_pieces(c):
    p0 = c.astype(BF16)
    r0 = c - p0.astype(F32)
    p1 = r0.astype(BF16)
    p2 = (r0 - p1.astype(F32)).astype(BF16)
    return p0, p1, p2


def _cumsum_kernel(f_ref, c_ref):
    ch = CUM_CHUNK
    s = f_ref.shape[1]
    row = lax.broadcasted_iota(I32, (ch, ch), 0)
    col = lax.broadcasted_iota(I32, (ch, ch), 1)
    tri = (col <= row).astype(BF16)
    carry = jnp.zeros((1, LANES), F32)
    for i in range(s // ch):
        cs = carry
        for p in _bf16_pieces(f_ref[0, i * ch:(i + 1) * ch, :]):
            cs = cs + jnp.dot(tri, p, preferred_element_type=F32)
        c_ref[0, i * ch:(i + 1) * ch, :] = cs
        carry = cs[ch - 1:ch, :]


def _forget_cumsum(logf):
    b, s, _ = logf.shape
    return pl.pallas_call(
        _cumsum_kernel,
        grid=(b,),
        in_specs=[pl.BlockSpec((1, s, LANES), lambda bi: (bi, 0, 0))],
        out_specs=pl.BlockSpec((1, s, LANES), lambda bi: (bi, 0, 0)),
        out_shape=jax.ShapeDtypeStruct((b, s, LANES), F32),
        compiler_params=_params(1),
        name="forget_cumsum",
    )(logf)


def _attn_kernel(q_ref, k_ref, v_ref, c_ref, o_ref, kx_sc, qx_sc, vt_sc, s_sc, m_sc, acc_sc):
    tq = TQ
    n_t = k_ref.shape[1] // tq
    heads = range(HEADS_PER_STEP)
    sel_r = lax.broadcasted_iota(I32, (LANES, LANES), 0)
    sel_c = lax.broadcasted_iota(I32, (LANES, LANES), 1)
    lane1 = lax.broadcasted_iota(I32, (1, LANES), 1)

    def head_lanes(hh):
        return slice(hh * HEAD_DIM, (hh + 1) * HEAD_DIM)

    def spread(pieces, hh, first_lane, sign):
        head = pl.program_id(1) * HEADS_PER_STEP + hh
        out = jnp.zeros(pieces[0].shape, F32)
        for i, p in enumerate(pieces):
            sel = jnp.where((sel_r == head) & (sel_c == first_lane + i), sign, 0.0).astype(BF16)
            out = out + jnp.dot(p, sel, preferred_element_type=F32)
        return out

    ones_k = jnp.where((lane1 >= 3) & (lane1 < 6), 1.0, 0.0)
    ones_q = jnp.where(lane1 < 3, 1.0, 0.0)
    extra = lax.broadcasted_iota(I32, (ATTN_V_ROWS - HEAD_DIM, tq), 0)
    ones_row = jnp.where(extra == 0, 1.0, 0.0).astype(BF16)

    def fill(i, carry):
        st = pl.multiple_of(i * tq, tq)
        pieces = _bf16_pieces(c_ref[0, pl.ds(st, tq), :] * LOG2E)
        for hh in heads:
            kx_sc[hh, pl.ds(st, tq), :] = (spread(pieces, hh, 0, -1.0) + ones_k).astype(BF16)
            qx_sc[hh, pl.ds(st, tq), :] = (spread(pieces, hh, 3, 1.0) + ones_q).astype(BF16)
            v = v_ref[0, pl.ds(st, tq), head_lanes(hh)]
            vt_sc[hh, 0:HEAD_DIM, pl.ds(st, tq)] = v.astype(F32).T.astype(BF16)
            vt_sc[hh, HEAD_DIM:, pl.ds(st, tq)] = ones_row
        return carry

    lax.fori_loop(0, n_t, fill, 0)

    def reset():
        m_sc[...] = jnp.full(m_sc.shape, -jnp.inf, F32)
        acc_sc[...] = jnp.zeros(acc_sc.shape, F32)

    def scores_into(i, j, slot):
        sq = pl.multiple_of(i * tq, tq)
        sk = pl.multiple_of(j * tq, tq)
        for hh in heads:
            q_aug = jnp.concatenate(
                [q_ref[0, pl.ds(sq, tq), head_lanes(hh)], qx_sc[hh, pl.ds(sq, tq), :]], axis=-1)
            k_aug = jnp.concatenate(
                [k_ref[0, pl.ds(sk, tq), head_lanes(hh)], kx_sc[hh, pl.ds(sk, tq), :]], axis=-1)
            s_sc[slot, hh] = lax.dot_general(k_aug, q_aug, (((1,), (1,)), ((), ())),
                                             preferred_element_type=F32)

    def update(j, slot, masked):
        st = pl.multiple_of(j * tq, tq)
        for hh in heads:
            s = s_sc[slot, hh]
            if masked:
                key = lax.broadcasted_iota(I32, (tq, tq), 0)
                qry = lax.broadcasted_iota(I32, (tq, tq), 1)
                s = jnp.where(key <= qry, s, -jnp.inf)
            m_prev = m_sc[hh]
            m_new = jnp.maximum(m_prev, jnp.max(s, axis=0, keepdims=True))
            alpha = jnp.exp2(m_prev - m_new)
            p = jnp.exp2(s - m_new)
            pv = jnp.dot(vt_sc[hh, :, pl.ds(st, tq)], p.astype(BF16), preferred_element_type=F32)
            acc_sc[hh] = alpha * acc_sc[hh] + pv
            m_sc[hh] = m_new

    def finalize(i):
        sq = pl.multiple_of(i * tq, tq)
        for hh in heads:
            acc = acc_sc[hh]
            o_ref[0, pl.ds(sq, tq), head_lanes(hh)] = (
                acc[0:HEAD_DIM] / acc[HEAD_DIM:HEAD_DIM + 1]).T.astype(BF16)
        reset()

    def tile(i, j, slot):
        diag = j == i
        ni = jnp.where(diag, i + 1, i)
        nj = jnp.where(diag, 0, j + 1)
        si = jnp.minimum(ni, n_t - 1)
        sj = jnp.where(ni == n_t, n_t - 1, nj)

        @pl.when(diag)
        def _():
            scores_into(si, sj, 1 - slot)
            update(j, slot, True)
            finalize(i)

        @pl.when(jnp.logical_not(diag))
        def _():
            scores_into(si, sj, 1 - slot)
            update(j, slot, False)

        return ni, nj

    reset()
    scores_into(0, 0, 0)

    def body(t, ij):
        ij = tile(ij[0], ij[1], 0)
        return tile(ij[0], ij[1], 1)

    n_tiles = n_t * (n_t + 1) // 2
    assert n_tiles % 2 == 0
    lax.fori_loop(0, n_tiles // 2, body, (jnp.int32(0), jnp.int32(0)))


def _fox_attention(q, k, v, cum):
    b, s, _ = q.shape
    tq, hp = TQ, HEADS_PER_STEP
    head = pl.BlockSpec((1, s, hp * HEAD_DIM), lambda bi, g: (bi, 0, g))
    return pl.pallas_call(
        _attn_kernel,
        grid=(b, ATTN_HEADS // hp),
        in_specs=[head, head, head, pl.BlockSpec((1, s, LANES), lambda bi, g: (bi, 0, 0))],
        out_specs=head,
        out_shape=jax.ShapeDtypeStruct((b, s, ATTN_WIDTH), BF16),
        scratch_shapes=[pltpu.VMEM((hp, s, LANES), BF16),
                        pltpu.VMEM((hp, s, LANES), BF16),
                        pltpu.VMEM((hp, ATTN_V_ROWS, s), BF16),
                        pltpu.VMEM((2, hp, tq, tq), F32),
                        pltpu.VMEM((hp, 1, tq), F32),
                        pltpu.VMEM((hp, ATTN_V_ROWS, tq), F32)],
        compiler_params=_params(2),
        name="fox_attention",
    )(q, k, v, cum)


def _conv_kernel(u_ref, w_ref, cb_ref, lg_ref, lb_ref, o_ref, ubuf, shifted, wb):
    i = pl.program_id(1)
    ts, halo, ck = TS_CONV, CONV_HALO, CONV_CHUNK

    @pl.when(i == 0)
    def _():
        ubuf[0:halo, :] = jnp.zeros((halo, CONV_WIDTH), F32)
        for j in range(CONV_KERNEL):
            wb[j * SUBLANES:(j + 1) * SUBLANES, :] = jnp.broadcast_to(
                w_ref[j:j + 1, :], (SUBLANES, CONV_WIDTH))

    ubuf[halo:halo + ts, :] = u_ref[0].astype(F32)
    base = halo - (CONV_KERNEL - 1)
    span = shifted.shape[1]
    for r in range(1, SUBLANES):
        shifted[r] = ubuf[r:r + span, :]
    for c in range(ts // ck):
        acc = jnp.zeros((ck, CONV_WIDTH), F32)
        for j in range(CONV_KERNEL):
            off = c * ck + base + j
            r, al = off % SUBLANES, off - off % SUBLANES
            slab = ubuf[al:al + ck, :] if r == 0 else shifted[r, al:al + ck, :]
            wj = wb[j * SUBLANES:(j + 1) * SUBLANES, :]
            acc = acc + slab * jnp.concatenate([wj] * (ck // SUBLANES), axis=0)
        y = acc + cb_ref[...]
        mu = jnp.mean(y, axis=-1, keepdims=True)
        yc = y - mu
        var = jnp.mean(yc * yc, axis=-1, keepdims=True)
        z = yc * lax.rsqrt(var + EPS) * lg_ref[...] + lb_ref[...]
        o_ref[0, c * ck:(c + 1) * ck, :] = (z * jax.nn.sigmoid(z)).astype(BF16)
    ubuf[0:halo, :] = ubuf[ts:ts + halo, :]


def _conv_module(u, conv_w, conv_b, ln_g, ln_b):
    b, s, cw = u.shape
    ts = TS_CONV
    return pl.pallas_call(
        _conv_kernel,
        grid=(b, s // ts),
        in_specs=[pl.BlockSpec((1, ts, cw), lambda bi, i: (bi, i, 0)),
                  _resident((CONV_KERNEL, cw)), _resident((1, cw)),
                  _resident((1, cw)), _resident((1, cw))],
        out_specs=pl.BlockSpec((1, ts, cw), lambda bi, i: (bi, i, 0)),
        out_shape=jax.ShapeDtypeStruct((b, s, cw), BF16),
        scratch_shapes=[pltpu.VMEM((CONV_HALO + ts, cw), F32),
                        pltpu.VMEM((SUBLANES, ts + CONV_HALO - SUBLANES, cw), F32),
                        pltpu.VMEM((CONV_KERNEL * SUBLANES, cw), F32)],
        compiler_params=_params(2),
        name="conv_module",
    )(u, conv_w, conv_b, ln_g, ln_b)


def _pack_rows(val):
    lo = lax.bitcast_convert_type(val[:, :HALF].astype(BF16).astype(F32), U32)
    hi = lax.bitcast_convert_type(val[:, HALF:].astype(BF16).astype(F32), U32)
    return hi | (lo >> 16)


def _unpack_rows(word):
    lo = lax.bitcast_convert_type(word << 16, F32)
    hi = lax.bitcast_convert_type(word & jnp.uint32(0xFFFF0000), F32)
    return lo, hi


def _store_row_slabs(ref, word):
    rows = word.shape[0]
    for s in range(ROW_SLAB):
        ref[pl.ds(s, rows, stride=ROW_SLAB), :] = word[:, s * LANES:(s + 1) * LANES]


def _load_row_slabs(ref, start, rows):
    parts = [ref[pl.ds(start * ROW_SLAB + s, rows, stride=ROW_SLAB), :] for s in range(ROW_SLAB)]
    return jnp.concatenate(parts, axis=-1)


def _outproj_kernel(at_ref, cv_ref, x_ref, wo_ref, g1_ref, a2_ref, s2_ref, wr_ref, br_ref,
                    x1_ref, z_ref, lg_ref):
    mix = (jnp.dot(at_ref[0], wo_ref[0:ATTN_WIDTH, :], preferred_element_type=F32)
           + jnp.dot(cv_ref[0], wo_ref[ATTN_WIDTH:, :], preferred_element_type=F32))
    x1 = x_ref[0] + g1_ref[0] * mix
    x1_ref[0] = x1
    h2 = x1 * lax.rsqrt(jnp.mean(x1 * x1, axis=-1, keepdims=True) + EPS) * a2_ref[0] + s2_ref[0]
    _store_row_slabs(z_ref, _pack_rows(h2))
    lg_ref[0] = jnp.dot(h2.astype(BF16), wr_ref[...], preferred_element_type=F32) + br_ref[...]


def _out_proj(attn, conv, x, w_out, g1, a2, s2, w_r, b_r):
    b, s, d = x.shape
    tm = TM_PROJ
    nt = s // tm
    tok = lambda w: pl.BlockSpec((1, tm, w), lambda bi, i: (bi, i, 0))
    per_batch = pl.BlockSpec((1, 1, d), lambda bi, i: (bi, 0, 0))
    return pl.pallas_call(
        _outproj_kernel,
        grid=(b, nt),
        in_specs=[tok(ATTN_WIDTH), tok(CONV_WIDTH), tok(d), _resident((d, d)),
                  per_batch, per_batch, per_batch,
                  _resident((d, LANES)), _resident((1, LANES))],
        out_specs=[tok(d),
                   pl.BlockSpec((tm * ROW_SLAB, LANES), lambda bi, i: (bi * nt + i, 0)),
                   tok(LANES)],
        out_shape=[jax.ShapeDtypeStruct((b, s, d), F32),
                   jax.ShapeDtypeStruct((b * s * ROW_SLAB, LANES), U32),
                   jax.ShapeDtypeStruct((b, s, LANES), F32)],
        compiler_params=_params(2),
        name="out_proj",
    )(attn, conv, x, w_out, g1, a2, s2, w_r, b_r)


def _route_kernel(lg_ref, ri_ref, rr_ref, rw_ref, cnt_ref, carry):
    i = pl.program_id(0)
    tm = TM_ROUTE

    @pl.when(i == 0)
    def _():
        carry[...] = jnp.zeros(carry.shape, F32)

    l = lg_ref[...]
    lane = lax.broadcasted_iota(I32, (tm, LANES), 1).astype(F32)
    vals, idxs = [], []
    for _ in range(TOP_K):
        m = jnp.max(l, axis=-1, keepdims=True)
        ix = jnp.min(jnp.where(l == m, lane, float(LANES)), axis=-1, keepdims=True)
        vals.append(m)
        idxs.append(ix)
        l = jnp.where(lane == ix, -jnp.inf, l)
    es = [jnp.exp(v - vals[0]) for v in vals]
    den = es[0] + es[1] + es[2] + es[3]
    onehot = jnp.zeros((tm, LANES), F32)
    for ix in idxs:
        onehot = onehot + jnp.where(lane == ix, 1.0, 0.0)
    row = lax.broadcasted_iota(I32, (tm, tm), 0)
    col = lax.broadcasted_iota(I32, (tm, tm), 1)
    strict = (col < row).astype(BF16)
    before = jnp.dot(strict, onehot.astype(BF16), preferred_element_type=F32) + carry[0:1, :]
    ri = jnp.zeros((tm, LANES), F32)
    rr = jnp.zeros((tm, LANES), F32)
    rw = jnp.zeros((tm, LANES), F32)
    for k in range(TOP_K):
        rank = jnp.sum(jnp.where(lane == idxs[k], before, 0.0), axis=-1, keepdims=True)
        ri = jnp.where(lane == k, idxs[k], ri)
        rr = jnp.where(lane == k, rank, rr)
        rw = jnp.where(lane == k, es[k] / den, rw)
    ri_ref[...] = ri.astype(I32)
    rr_ref[...] = rr.astype(I32)
    rw_ref[...] = rw
    carry[0:1, :] = carry[0:1, :] + jnp.sum(onehot, axis=0, keepdims=True)
    cnt_ref[...] = carry[...]


def _route(logits):
    t = logits.shape[0]
    tm = TM_ROUTE
    return pl.pallas_call(
        _route_kernel,
        grid=(t // tm,),
        in_specs=[pl.BlockSpec((tm, LANES), lambda i: (i, 0))],
        out_specs=[pl.BlockSpec((tm, LANES), lambda i: (i, 0)),
                   pl.BlockSpec((tm, LANES), lambda i: (i, 0)),
                   pl.BlockSpec((tm, LANES), lambda i: (i, 0)),
                   pl.BlockSpec((SUBLANES, LANES), lambda i: (0, 0))],
        out_shape=[jax.ShapeDtypeStruct((t, LANES), I32),
                   jax.ShapeDtypeStruct((t, LANES), I32),
                   jax.ShapeDtypeStruct((t, LANES), F32),
                   jax.ShapeDtypeStruct((SUBLANES, LANES), F32)],
        scratch_shapes=[pltpu.VMEM((SUBLANES, LANES), F32)],
        compiler_params=_params(1),
        name="route",
    )(logits)


def _issue_row_gather(idx_ref, idx_base, n_rows, src_hbm, dst_buf, sem):
    def body(g, carry):
        for u in range(GATHER_UNROLL):
            r = g * GATHER_UNROLL + u
            t = idx_ref[idx_base + r]
            pltpu.make_async_copy(
                src_hbm.at[pl.ds(pl.multiple_of(t * ROW_SLAB, ROW_SLAB), ROW_SLAB), :],
                dst_buf.at[pl.ds(pl.multiple_of(r * ROW_SLAB, ROW_SLAB), ROW_SLAB), :],
                sem).start(priority=u % 2)
        return carry
    lax.fori_loop(0, n_rows // GATHER_UNROLL, body, 0)


def _wait_row_gather(n_rows, src_hbm, dst_buf, sem):
    pltpu.make_async_copy(src_hbm.at[pl.ds(0, n_rows * ROW_SLAB), :], dst_buf, sem).wait()


def _dispatch_kernel(dest_ref, pad_ref, z_ref, xs_hbm, stage, zeros, sem, zsem):
    i = pl.program_id(0)
    n = pl.num_programs(0)
    tm = TM_DISP
    pad_rows = TM_MOE * ROW_SLAB

    def zero_copy(slot, n_slots):
        start = pl.multiple_of(slot * ROW_SLAB, ROW_SLAB)
        return pltpu.make_async_copy(zeros.at[pl.ds(0, n_slots * ROW_SLAB), :],
                                     xs_hbm.at[pl.ds(start, n_slots * ROW_SLAB), :], zsem)

    def for_pad_pieces(e, fn):
        slot = pad_ref[e]
        n = pad_ref[N_EXPERTS + e]
        size = TM_MOE // 2
        while size >= 1:
            piece = zero_copy(slot, size)
            pl.when((n & size) != 0)(functools.partial(fn, piece))
            slot = slot + (n & size)
            size //= 2

    @pl.when(i == 0)
    def _():
        zeros[...] = jnp.zeros(zeros.shape, U32)
        first_free = pad_ref[2 * N_EXPERTS]
        n_total = xs_hbm.shape[0] // pad_rows

        def start_all(e, carry):
            for_pad_pieces(e, lambda piece: piece.start())
            return carry

        def wait_all(e, carry):
            for_pad_pieces(e, lambda piece: piece.wait())
            return carry

        def start_free(b, carry):
            zero_copy(b * TM_MOE, TM_MOE).start()
            return carry

        def wait_free(b, carry):
            zero_copy(b * TM_MOE, TM_MOE).wait()
            return carry

        lax.fori_loop(0, N_EXPERTS, start_all, 0)
        lax.fori_loop(first_free, n_total, start_free, 0)
        lax.fori_loop(0, N_EXPERTS, wait_all, 0)
        lax.fori_loop(first_free, n_total, wait_free, 0)

    cur = i % 2
    stage[cur] = z_ref[...]

    def body(g, carry):
        for u in range(GATHER_UNROLL // TOP_K):
            r = g * (GATHER_UNROLL // TOP_K) + u
            src = stage.at[cur, pl.ds(pl.multiple_of(r * ROW_SLAB, ROW_SLAB), ROW_SLAB), :]
            for k in range(TOP_K):
                d = dest_ref[(i * tm + r) * TOP_K + k]
                pltpu.make_async_copy(
                    src, xs_hbm.at[pl.ds(pl.multiple_of(d * ROW_SLAB, ROW_SLAB), ROW_SLAB), :],
                    sem.at[cur]).start(priority=k % 2)
        return carry

    lax.fori_loop(0, tm * TOP_K // GATHER_UNROLL, body, 0)

    def drain(slot):
        for _ in range(TOP_K):
            pltpu.make_async_copy(stage.at[slot], xs_hbm.at[pl.ds(0, tm * ROW_SLAB), :], sem.at[slot]).wait()

    @pl.when(i > 0)
    def _():
        drain(1 - cur)

    @pl.when(i == n - 1)
    def _():
        drain(cur)


def _dispatch(dest_flat, pad_start, z, n_slots):
    t = z.shape[0] // ROW_SLAB
    tm = TM_DISP
    return pl.pallas_call(
        _dispatch_kernel,
        grid_spec=pltpu.PrefetchScalarGridSpec(
            num_scalar_prefetch=2,
            grid=(t // tm,),
            in_specs=[pl.BlockSpec((tm * ROW_SLAB, LANES), lambda i, dr, pr: (i, 0))],
            out_specs=pl.BlockSpec(memory_space=pl.ANY),
            scratch_shapes=[pltpu.VMEM((2, tm * ROW_SLAB, LANES), U32),
                            pltpu.VMEM((TM_MOE * ROW_SLAB, LANES), U32),
                            pltpu.SemaphoreType.DMA((2,)), pltpu.SemaphoreType.DMA(())]),
        out_shape=jax.ShapeDtypeStruct((n_slots * ROW_SLAB, LANES), U32),
        compiler_params=_params(1),
        name="dispatch",
    )(dest_flat, pad_start, z)


SCHED_W = 5
S_EXPERT, S_SLOT, S_NEXT, S_C0, S_C1 = range(SCHED_W)


def _expert_weights_step(sched_ref, b, n_used, w_hbm, wbuf, stage, sems, compute):
    n_mat = len(w_hbm)
    n_chunk = wbuf.shape[2] // W_CHUNK

    def rows(rc):
        return pl.ds(pl.multiple_of(rc * W_CHUNK, W_CHUNK), W_CHUNK)

    def copies(e, rc, j):
        return [pltpu.make_async_copy(w_hbm[m].at[e, rows(rc), :], stage.at[j, m], sems.at[j, m])
                for m in range(n_mat)]

    def convert(slot, rc, j):
        for m in range(n_mat):
            wbuf[slot, m, rows(rc), :] = stage[j, m].astype(BF16)

    def load_now(e, slot, rc0, rc1):
        def body(rc, carry):
            for cp in copies(e, rc, 0):
                cp.start()
            for cp in copies(e, rc, 0):
                cp.wait()
            convert(slot, rc, 0)
            return carry
        lax.fori_loop(rc0, rc1, body, 0)

    @pl.when(b == 0)
    def _():
        load_now(sched_ref[S_EXPERT], 0, 0, n_chunk)

    @pl.when(b < n_used)
    def _():
        base = b * SCHED_W
        slot, nxt = sched_ref[base + S_SLOT], sched_ref[base + S_NEXT]
        c0, c1 = sched_ref[base + S_C0], sched_ref[base + S_C1]
        def start_chunk(c, carry):
            for cp in copies(nxt, c, c - c0):
                cp.start()
            return carry
        lax.fori_loop(c0, jnp.minimum(c0 + N_STAGE, c1), start_chunk, 0)
        compute(slot)
        for j in range(N_STAGE):
            @pl.when(c0 + j < c1)
            def _():
                for cp in copies(nxt, c0 + j, j):
                    cp.wait()
                convert(1 - slot, c0 + j, j)
        load_now(nxt, 1 - slot, jnp.minimum(c0 + N_STAGE, c1), c1)


def _moe_up_kernel(sched_ref, nu_ref, xs_ref, wg_hbm, bg_ref, wu_hbm, bu_ref, hid_ref, wbuf, stage, sems):
    b = pl.program_id(0)
    tm = TM_MOE

    def compute(slot):
        lo, hi = _unpack_rows(_load_row_slabs(xs_ref, 0, tm))
        x = jnp.concatenate([lo.astype(BF16), hi.astype(BF16)], axis=-1)
        g = jnp.dot(x, wbuf[slot, 0], preferred_element_type=F32) + bg_ref[0]
        u = jnp.dot(x, wbuf[slot, 1], preferred_element_type=F32) + bu_ref[0]
        g = jnp.minimum(g, SWIGLU_LIMIT)
        u = jnp.clip(u, -SWIGLU_LIMIT, SWIGLU_LIMIT)
        hid_ref[...] = ((u + 1.0) * (g * jax.nn.sigmoid(SWIGLU_ALPHA * g))).astype(BF16)

    _expert_weights_step(sched_ref, b, nu_ref[0], (wg_hbm, wu_hbm), wbuf, stage, sems, compute)

    @pl.when(b >= nu_ref[0])
    def _():
        hid_ref[...] = jnp.zeros(hid_ref.shape, BF16)


def _expert_scratch(n_mat, k, n):
    return [pltpu.VMEM((2, n_mat, k, n), BF16),
            pltpu.VMEM((N_STAGE, n_mat, W_CHUNK, n), F32),
            pltpu.SemaphoreType.DMA((N_STAGE, n_mat))]


def _moe_up(sched, n_used, xs, w_gate, b_gate, w_up, b_up):
    e, d, f = w_gate.shape
    n_blocks = sched.shape[0] // SCHED_W
    tm = TM_MOE
    hbm = pl.BlockSpec(memory_space=pl.ANY)
    bspec = pl.BlockSpec((1, 1, f), lambda b, sc, nu: (sc[b * SCHED_W + S_EXPERT], 0, 0))
    return pl.pallas_call(
        _moe_up_kernel,
        grid_spec=pltpu.PrefetchScalarGridSpec(
            num_scalar_prefetch=2,
            grid=(n_blocks,),
            in_specs=[pl.BlockSpec((tm * ROW_SLAB, LANES), lambda b, sc, nu: (jnp.minimum(b, nu[0] - 1), 0)),
                      hbm, bspec, hbm, bspec],
            out_specs=pl.BlockSpec((tm, f), lambda b, sc, nu: (b, 0)),
            scratch_shapes=_expert_scratch(2, d, f)),
        out_shape=jax.ShapeDtypeStruct((n_blocks * tm, f), BF16),
        compiler_params=_params(1),
        name="moe_up",
    )(sched, n_used, xs, w_gate, b_gate, w_up, b_up)


def _moe_down_kernel(sched_ref, nu_ref, hid_ref, wd_hbm, bd_ref, ys_ref, wbuf, stage, sems):
    b = pl.program_id(0)

    def compute(slot):
        out = jnp.dot(hid_ref[...], wbuf[slot, 0], preferred_element_type=F32) + bd_ref[0]
        _store_row_slabs(ys_ref, _pack_rows(out))

    _expert_weights_step(sched_ref, b, nu_ref[0], (wd_hbm,), wbuf, stage, sems, compute)

    @pl.when(b >= nu_ref[0])
    def _():
        ys_ref[...] = jnp.zeros(ys_ref.shape, U32)


def _moe_down(sched, n_used, hid, w_down, b_down):
    e, f, d = w_down.shape
    n_blocks = sched.shape[0] // SCHED_W
    tm = TM_MOE
    return pl.pallas_call(
        _moe_down_kernel,
        grid_spec=pltpu.PrefetchScalarGridSpec(
            num_scalar_prefetch=2,
            grid=(n_blocks,),
            in_specs=[pl.BlockSpec((tm, f), lambda b, sc, nu: (b, 0)),
                      pl.BlockSpec(memory_space=pl.ANY),
                      pl.BlockSpec((1, 1, d), lambda b, sc, nu: (sc[b * SCHED_W + S_EXPERT], 0, 0))],
            out_specs=pl.BlockSpec((tm * ROW_SLAB, LANES), lambda b, sc, nu: (b, 0)),
            scratch_shapes=_expert_scratch(1, f, d)),
        out_shape=jax.ShapeDtypeStruct((n_blocks * tm * ROW_SLAB, LANES), U32),
        compiler_params=_params(1),
        name="moe_down",
    )(sched, n_used, hid, w_down, b_down)


def _combine_kernel(dest_ref, ys_hbm, x1_ref, rw_ref, g2_ref, o_ref, buf, sem):
    i = pl.program_id(0)
    n = pl.num_programs(0)
    tm = TM_COMB
    rows = TOP_K * tm

    @pl.when(i == 0)
    def _():
        _issue_row_gather(dest_ref, 0, rows, ys_hbm, buf.at[0], sem.at[0])

    @pl.when(i + 1 < n)
    def _():
        nxt = (i + 1) % 2
        _issue_row_gather(dest_ref, (i + 1) * rows, rows, ys_hbm, buf.at[nxt], sem.at[nxt])

    cur = i % 2
    _wait_row_gather(rows, ys_hbm, buf.at[cur], sem.at[cur])
    rw = rw_ref[...]
    y_lo = jnp.zeros((tm, HALF), F32)
    y_hi = jnp.zeros((tm, HALF), F32)
    for k in range(TOP_K):
        lo, hi = _unpack_rows(_load_row_slabs(buf.at[cur], k * tm, tm))
        y_lo = y_lo + rw[:, k:k + 1] * lo
        y_hi = y_hi + rw[:, k:k + 1] * hi
    o_ref[...] = x1_ref[...] + g2_ref[0] * jnp.concatenate([y_lo, y_hi], axis=-1)


def _combine(dest_km, ys, x1, rw, g2, seq):
    t, d = x1.shape
    tm = TM_COMB
    per_seq = seq // tm
    return pl.pallas_call(
        _combine_kernel,
        grid_spec=pltpu.PrefetchScalarGridSpec(
            num_scalar_prefetch=1,
            grid=(t // tm,),
            in_specs=[pl.BlockSpec(memory_space=pl.ANY),
                      pl.BlockSpec((tm, d), lambda i, dr: (i, 0)),
                      pl.BlockSpec((tm, LANES), lambda i, dr: (i, 0)),
                      pl.BlockSpec((1, 1, d), lambda i, dr: (i // per_seq, 0, 0))],
            out_specs=pl.BlockSpec((tm, d), lambda i, dr: (i, 0)),
            scratch_shapes=[pltpu.VMEM((2, TOP_K * tm * ROW_SLAB, LANES), U32),
                            pltpu.SemaphoreType.DMA((2,))]),
        out_shape=jax.ShapeDtypeStruct((t, d), F32),
        compiler_params=_params(1),
        name="combine",
    )(dest_km, ys, x1, rw, g2)


def _pad_cols(w, n):
    return jnp.pad(w, ((0, 0), (0, n - w.shape[1])))


def _layer(x, mod, norm_mix_g, norm_ffn_g, w_in, b_f, q_norm_g, k_norm_g, conv_w, conv_b,
           conv_ln_g, conv_ln_b, w_out, w_router, b_router, w_gate, b_gate, w_up, b_up,
           w_down, b_down):
    b, s, d = x.shape
    t = b * s
    shift1, scale1, gate1, shift2, scale2, gate2 = [m[:, None, :] for m in jnp.split(mod, 6, axis=-1)]
    a1 = norm_mix_g[None, None, :] * (1.0 + scale1)
    a2 = norm_ffn_g[None, None, :] * (1.0 + scale2)

    aw = ATTN_WIDTH
    w_cat = jnp.concatenate([w_in[:, :3 * aw], w_in[:, 3 * aw + ATTN_HEADS:]], axis=1).astype(BF16)
    w_f = _pad_cols(w_in[:, 3 * aw:3 * aw + ATTN_HEADS], LANES).astype(BF16)
    b_f_pad = _pad_cols(b_f[None, :], LANES)

    q, k, v, u, logf = _in_proj(x, a1, shift1, w_cat, w_f, b_f_pad,
                                q_norm_g[None, :], k_norm_g[None, :])
    attn = _fox_attention(q, k, v, _forget_cumsum(logf))
    conv = _conv_module(u, conv_w, conv_b[None, :], conv_ln_g[None, :], conv_ln_b[None, :])

    w_r = _pad_cols(w_router, LANES).astype(BF16)
    b_r = jnp.concatenate([b_router, jnp.full((LANES - N_EXPERTS,), NEG_BIG, F32)])[None, :]
    x1, z, logits = _out_proj(attn, conv, x, w_out.astype(BF16), gate1, a2, shift2, w_r, b_r)

    ri, rr, rw, cnt = _route(logits.reshape(t, LANES))

    tm = TM_MOE
    n_blocks = t * TOP_K // tm + N_EXPERTS
    counts = cnt[0, :N_EXPERTS].astype(I32)
    padded = (counts + tm - 1) // tm * tm
    padded_end = jnp.cumsum(padded)
    padded_start = padded_end - padded
    start_of = jnp.zeros_like(ri)
    for e in range(N_EXPERTS):
        start_of = jnp.where(ri == e, padded_start[e], start_of)
    dest = (start_of + rr)[:, :TOP_K]
    block_start = jnp.arange(n_blocks, dtype=I32)[:, None] * tm
    block_e = jnp.minimum(jnp.sum((padded_end[None, :] <= block_start).astype(I32), axis=1), N_EXPERTS - 1)
    n_used = (padded_end[-1:] // tm).astype(I32)
    dest_km = dest.reshape(t // TM_COMB, TM_COMB, TOP_K).transpose(0, 2, 1).reshape(-1)

    pad_table = jnp.concatenate([padded_start + counts, padded - counts, n_used])
    xs = _dispatch(dest.reshape(-1), pad_table, z, n_blocks * tm)

    experts = jnp.arange(N_EXPERTS, dtype=I32)
    has = counts > 0
    later = (experts[None, :] > experts[:, None]) & has[None, :]
    next_e = jnp.min(jnp.where(later, experts[None, :], N_EXPERTS), axis=1)
    run_slot = (jnp.cumsum(has.astype(I32)) - 1) % 2
    of_block = block_e[:, None] == experts[None, :]

    def per_block(table):
        return jnp.sum(jnp.where(of_block, table[None, :], 0), axis=1)

    run_len = per_block(jnp.maximum(padded // tm, 1))
    pos = jnp.arange(n_blocks, dtype=I32) - per_block(padded_start // tm)
    nxt = per_block(next_e)
    live = nxt < N_EXPERTS
    c0 = jnp.where(live, N_WCHUNK * pos // run_len, 0)
    c1 = jnp.where(live, N_WCHUNK * (pos + 1) // run_len, 0)
    sched = jnp.stack([block_e, per_block(run_slot), jnp.where(live, nxt, 0), c0, c1], axis=1).reshape(-1)

    hid = _moe_up(sched, n_used, xs, w_gate, b_gate[:, None, :], w_up, b_up[:, None, :])
    ys = _moe_down(sched, n_used, hid, w_down, b_down[:, None, :])
    out = _combine(dest_km, ys, x1.reshape(t, d), rw, gate2, s)
    return out.reshape(b, s, d)


def kernel(x, c, ada_w, ada_b, norm_mix_g, norm_ffn_g, w_in, b_f, q_norm_g, k_norm_g, conv_w, conv_b,
           conv_ln_g, conv_ln_b, w_out, w_router, b_router, w_gate, b_gate, w_up, b_up, w_down, b_down):
    b = x.shape[0]
    c_pad = jnp.pad(c, ((0, SUBLANES - b), (0, 0)))
    for l in range(ada_w.shape[0]):
        mod = _ada_mod(c_pad, ada_w[l], ada_b[l])[:b]
        x = _layer(x, mod, norm_mix_g[l], norm_ffn_g[l], w_in[l], b_f[l], q_norm_g[l], k_norm_g[l],
                   conv_w[l], conv_b[l], conv_ln_g[l], conv_ln_b[l], w_out[l], w_router[l],
                   b_router[l], w_gate[l], b_gate[l], w_up[l], b_up[l], w_down[l], b_down[l])
    return x
```

```python
import functools

import jax
import jax.numpy as jnp
from jax import lax
from jax.experimental import pallas as pl
from jax.experimental.pallas import tpu as pltpu

F32 = jnp.float32
BF16 = jnp.bfloat16
I32 = jnp.int32
U32 = jnp.uint32

D_MODEL = 2048
ATTN_HEADS = 8
HEAD_DIM = 128
ATTN_WIDTH = ATTN_HEADS * HEAD_DIM
CONV_WIDTH = D_MODEL - ATTN_WIDTH
CONV_KERNEL = 31
N_EXPERTS = 32
TOP_K = 4
SWIGLU_LIMIT = 7.0
SWIGLU_ALPHA = 1.702
EPS = 1e-6
LOG2E = 1.4426950408889634

LANES = 128
SUBLANES = 8
HALF = D_MODEL // 2
ROW_SLAB = HALF // LANES
VMEM_LIMIT = 56 * 1024 * 1024

TM_PROJ = 512
TQ = 512
ATTN_V_ROWS = HEAD_DIM + 16
HEADS_PER_STEP = 2
CUM_CHUNK = 256
TS_CONV = 256
CONV_HALO = 32
CONV_CHUNK = 32
TM_ROUTE = 512
TM_MOE = 256
TM_DISP = 256
TM_COMB = 256
GATHER_UNROLL = 32
W_CHUNK = 256
N_WCHUNK = D_MODEL // W_CHUNK
N_STAGE = 2
NEG_BIG = -1e30


def _params(n_axes):
    return pltpu.CompilerParams(
        dimension_semantics=("arbitrary",) * n_axes, vmem_limit_bytes=VMEM_LIMIT)


def _resident(shape):
    nd = len(shape)
    return pl.BlockSpec(shape, lambda *_: (0,) * nd, pipeline_mode=pl.Buffered(1))


def _ada_kernel(c_ref, w_ref, b_ref, o_ref):
    c = c_ref[...]
    c_act = (c * jax.nn.sigmoid(c)).astype(BF16)
    o_ref[...] = jnp.dot(c_act, w_ref[...].astype(BF16), preferred_element_type=F32) + b_ref[...]


def _ada_mod(c_pad, ada_w, ada_b):
    rows, d = c_pad.shape
    n = ada_w.shape[1]
    tn = 1024
    return pl.pallas_call(
        _ada_kernel,
        grid=(n // tn,),
        in_specs=[pl.BlockSpec((rows, d), lambda j: (0, 0)),
                  pl.BlockSpec((d, tn), lambda j: (0, j)),
                  pl.BlockSpec((1, tn), lambda j: (0, j))],
        out_specs=pl.BlockSpec((rows, tn), lambda j: (0, j)),
        out_shape=jax.ShapeDtypeStruct((rows, n), F32),
        compiler_params=_params(1),
        name="ada_mod",
    )(c_pad, ada_w, ada_b.reshape(1, n))


def _log_sigmoid(x):
    return jnp.minimum(x, 0.0) - jnp.log1p(jnp.exp(-jnp.abs(x)))


def _head_rms(y, g):
    outs = []
    for h in range(ATTN_HEADS):
        yh = y[:, h * HEAD_DIM:(h + 1) * HEAD_DIM]
        r = lax.rsqrt(jnp.mean(yh * yh, axis=-1, keepdims=True) + EPS)
        outs.append(yh * r * g)
    return jnp.concatenate(outs, axis=-1)


def _inproj_kernel(x_ref, a_ref, s_ref, w_ref, wc_ref, wf_ref, bf_ref, qg_ref, kg_ref,
                   q_ref, k_ref, v_ref, u_ref, f_ref):
    x = x_ref[0]
    h = x * lax.rsqrt(jnp.mean(x * x, axis=-1, keepdims=True) + EPS) * a_ref[0] + s_ref[0]
    hb = h.astype(BF16)
    aw = ATTN_WIDTH
    q = jnp.dot(hb, w_ref[:, 0:aw], preferred_element_type=F32)
    q_ref[0] = (_head_rms(q, qg_ref[...]) * (LOG2E * HEAD_DIM ** -0.5)).astype(BF16)
    k = jnp.dot(hb, w_ref[:, aw:2 * aw], preferred_element_type=F32)
    k_ref[0] = _head_rms(k, kg_ref[...]).astype(BF16)
    v_ref[0] = jnp.dot(hb, w_ref[:, 2 * aw:3 * aw], preferred_element_type=F32).astype(BF16)
    a = jnp.dot(hb, wc_ref[:, 0:CONV_WIDTH], preferred_element_type=F32)
    g = jnp.dot(hb, wc_ref[:, CONV_WIDTH:], preferred_element_type=F32)
    u_ref[0] = (a * jax.nn.sigmoid(g)).astype(BF16)
    fl = jnp.dot(hb, wf_ref[...], preferred_element_type=F32) + bf_ref[...]
    f_ref[0] = _log_sigmoid(fl)


def _in_proj(x, a1, s1, w_qkv, w_conv, w_f, b_f, q_g, k_g):
    b, s, d = x.shape
    tm = TM_PROJ
    tok = lambda w: pl.BlockSpec((1, tm, w), lambda bi, i: (bi, i, 0))
    per_batch = pl.BlockSpec((1, 1, d), lambda bi, i: (bi, 0, 0))
    return pl.pallas_call(
        _inproj_kernel,
        grid=(b, s // tm),
        in_specs=[tok(d), per_batch, per_batch,
                  _resident(w_qkv.shape), _resident(w_conv.shape), _resident((d, LANES)), _resident((1, LANES)),
                  _resident((1, HEAD_DIM)), _resident((1, HEAD_DIM))],
        out_specs=[tok(ATTN_WIDTH), tok(ATTN_WIDTH), tok(ATTN_WIDTH), tok(CONV_WIDTH), tok(LANES)],
        out_shape=[jax.ShapeDtypeStruct((b, s, ATTN_WIDTH), BF16)] * 3
        + [jax.ShapeDtypeStruct((b, s, CONV_WIDTH), BF16),
           jax.ShapeDtypeStruct((b, s, LANES), F32)],
        compiler_params=_params(2),
        name="in_proj",
    )(x, a1, s1, w_qkv, w_conv, w_f, b_f, q_g, k_g)


def _bf16_pieces(c):
    p0 = c.astype(BF16)
    r0 = c - p0.astype(F32)
    p1 = r0.astype(BF16)
    p2 = (r0 - p1.astype(F32)).astype(BF16)
    return p0, p1, p2


def _cumsum_kernel(f_ref, c_ref):
    ch = CUM_CHUNK
    s = f_ref.shape[1]
    row = lax.broadcasted_iota(I32, (ch, ch), 0)
    col = lax.broadcasted_iota(I32, (ch, ch), 1)
    tri = (col <= row).astype(BF16)
    carry = jnp.zeros((1, LANES), F32)
    for i in range(s // ch):
        cs = carry
        for p in _bf16_pieces(f_ref[0, i * ch:(i + 1) * ch, :]):
            cs = cs + jnp.dot(tri, p, preferred_element_type=F32)
        c_ref[0, i * ch:(i + 1) * ch, :] = cs
        carry = cs[ch - 1:ch, :]


def _forget_cumsum(logf):
    b, s, _ = logf.shape
    return pl.pallas_call(
        _cumsum_kernel,
        grid=(b,),
        in_specs=[pl.BlockSpec((1, s, LANES), lambda bi: (bi, 0, 0))],
        out_specs=pl.BlockSpec((1, s, LANES), lambda bi: (bi, 0, 0)),
        out_shape=jax.ShapeDtypeStruct((b, s, LANES), F32),
        compiler_params=_params(1),
        name="forget_cumsum",
    )(logf)


def _attn_kernel(q_ref, k_ref, v_ref, c_ref, o_ref, kx_sc, qx_sc, vt_sc, s_sc, m_sc, acc_sc):
    tq = TQ
    n_t = k_ref.shape[1] // tq
    heads = range(HEADS_PER_STEP)
    sel_r = lax.broadcasted_iota(I32, (LANES, LANES), 0)
    sel_c = lax.broadcasted_iota(I32, (LANES, LANES), 1)
    lane1 = lax.broadcasted_iota(I32, (1, LANES), 1)

    def head_lanes(hh):
        return slice(hh * HEAD_DIM, (hh + 1) * HEAD_DIM)

    def spread(pieces, hh, first_lane, sign):
        head = pl.program_id(1) * HEADS_PER_STEP + hh
        out = jnp.zeros(pieces[0].shape, F32)
        for i, p in enumerate(pieces):
            sel = jnp.where((sel_r == head) & (sel_c == first_lane + i), sign, 0.0).astype(BF16)
            out = out + jnp.dot(p, sel, preferred_element_type=F32)
        return out

    ones_k = jnp.where((lane1 >= 3) & (lane1 < 6), 1.0, 0.0)
    ones_q = jnp.where(lane1 < 3, 1.0, 0.0)
    extra = lax.broadcasted_iota(I32, (ATTN_V_ROWS - HEAD_DIM, tq), 0)
    ones_row = jnp.where(extra == 0, 1.0, 0.0).astype(BF16)

    def fill(i, carry):
        st = pl.multiple_of(i * tq, tq)
        pieces = _bf16_pieces(c_ref[0, pl.ds(st, tq), :] * LOG2E)
        for hh in heads:
            kx_sc[hh, pl.ds(st, tq), :] = (spread(pieces, hh, 0, -1.0) + ones_k).astype(BF16)
            qx_sc[hh, pl.ds(st, tq), :] = (spread(pieces, hh, 3, 1.0) + ones_q).astype(BF16)
            v = v_ref[0, pl.ds(st, tq), head_lanes(hh)]
            vt_sc[hh, 0:HEAD_DIM, pl.ds(st, tq)] = v.astype(F32).T.astype(BF16)
            vt_sc[hh, HEAD_DIM:, pl.ds(st, tq)] = ones_row
        return carry

    lax.fori_loop(0, n_t, fill, 0)

    def scores_into(i, j, slot):
        sq = pl.multiple_of(i * tq, tq)
        sk = pl.multiple_of(j * tq, tq)
        for hh in heads:
            q_aug = jnp.concatenate(
                [q_ref[0, pl.ds(sq, tq), head_lanes(hh)], qx_sc[hh, pl.ds(sq, tq), :]], axis=-1)
            k_aug = jnp.concatenate(
                [k_ref[0, pl.ds(sk, tq), head_lanes(hh)], kx_sc[hh, pl.ds(sk, tq), :]], axis=-1)
            s_sc[slot, hh] = lax.dot_general(k_aug, q_aug, (((1,), (1,)), ((), ())),
                                             preferred_element_type=F32)

    def reset():
        m_sc[...] = jnp.full(m_sc.shape, -jnp.inf, F32)
        acc_sc[...] = jnp.zeros(acc_sc.shape, F32)

    def update(j, slot, masked):
        st = pl.multiple_of(j * tq, tq)
        for hh in heads:
            s = s_sc[slot, hh]
            if masked:
                key = lax.broadcasted_iota(I32, (tq, tq), 0)
                qry = lax.broadcasted_iota(I32, (tq, tq), 1)
                s = jnp.where(key <= qry, s, -jnp.inf)
            m_prev = m_sc[hh]
            m_new = jnp.maximum(m_prev, jnp.max(s, axis=0, keepdims=True))
            alpha = jnp.exp2(m_prev - m_new)
            p = jnp.exp2(s - m_new)
            pv = jnp.dot(vt_sc[hh, :, pl.ds(st, tq)], p.astype(BF16), preferred_element_type=F32)
            acc_sc[hh] = alpha * acc_sc[hh] + pv
            m_sc[hh] = m_new

    def finalize(i):
        sq = pl.multiple_of(i * tq, tq)
        for hh in heads:
            acc = acc_sc[hh]
            o_ref[0, pl.ds(sq, tq), head_lanes(hh)] = (
                acc[0:HEAD_DIM] / acc[HEAD_DIM:HEAD_DIM + 1]).T.astype(BF16)
        reset()

    def tile(i, j, slot):
        diag = j == i
        ni = jnp.where(diag, i + 1, i)
        nj = jnp.where(diag, 0, j + 1)
        si = jnp.minimum(ni, n_t - 1)
        sj = jnp.where(ni == n_t, n_t - 1, nj)

        @pl.when(diag)
        def _():
            scores_into(si, sj, 1 - slot)
            update(j, slot, True)
            finalize(i)

        @pl.when(jnp.logical_not(diag))
        def _():
            scores_into(si, sj, 1 - slot)
            update(j, slot, False)

        return ni, nj

    reset()
    scores_into(0, 0, 0)

    def body(t, ij):
        ij = tile(ij[0], ij[1], 0)
        return tile(ij[0], ij[1], 1)

    n_tiles = n_t * (n_t + 1) // 2
    assert n_tiles % 2 == 0
    lax.fori_loop(0, n_tiles // 2, body, (jnp.int32(0), jnp.int32(0)))


def _fox_attention(q, k, v, cum):
    b, s, _ = q.shape
    tq, hp = TQ, HEADS_PER_STEP
    head = pl.BlockSpec((1, s, hp * HEAD_DIM), lambda bi, g: (bi, 0, g))
    return pl.pallas_call(
        _attn_kernel,
        grid=(b, ATTN_HEADS // hp),
        in_specs=[head, head, head, pl.BlockSpec((1, s, LANES), lambda bi, g: (bi, 0, 0))],
        out_specs=head,
        out_shape=jax.ShapeDtypeStruct((b, s, ATTN_WIDTH), BF16),
        scratch_shapes=[pltpu.VMEM((hp, s, LANES), BF16),
                        pltpu.VMEM((hp, s, LANES), BF16),
                        pltpu.VMEM((hp, ATTN_V_ROWS, s), BF16),
                        pltpu.VMEM((2, hp, tq, tq), F32),
                        pltpu.VMEM((hp, 1, tq), F32),
                        pltpu.VMEM((hp, ATTN_V_ROWS, tq), F32)],
        compiler_params=_params(2),
        name="fox_attention",
    )(q, k, v, cum)


def _conv_kernel(u_ref, w_ref, cb_ref, lg_ref, lb_ref, o_ref, ubuf, shifted, wb):
    i = pl.program_id(1)
    ts, halo, ck = TS_CONV, CONV_HALO, CONV_CHUNK

    @pl.when(i == 0)
    def _():
        ubuf[0:halo, :] = jnp.zeros((halo, CONV_WIDTH), F32)
        for j in range(CONV_KERNEL):
            wb[j * SUBLANES:(j + 1) * SUBLANES, :] = jnp.broadcast_to(
                w_ref[j:j + 1, :], (SUBLANES, CONV_WIDTH))

    ubuf[halo:halo + ts, :] = u_ref[0].astype(F32)
    base = halo - (CONV_KERNEL - 1)
    span = shifted.shape[1]
    for r in range(1, SUBLANES):
        shifted[r] = ubuf[r:r + span, :]
    for c in range(ts // ck):
        acc = jnp.zeros((ck, CONV_WIDTH), F32)
        for j in range(CONV_KERNEL):
            off = c * ck + base + j
            r, al = off % SUBLANES, off - off % SUBLANES
            slab = ubuf[al:al + ck, :] if r == 0 else shifted[r, al:al + ck, :]
            wj = wb[j * SUBLANES:(j + 1) * SUBLANES, :]
            acc = acc + slab * jnp.concatenate([wj] * (ck // SUBLANES), axis=0)
        y = acc + cb_ref[...]
        mu = jnp.mean(y, axis=-1, keepdims=True)
        yc = y - mu
        var = jnp.mean(yc * yc, axis=-1, keepdims=True)
        z = yc * lax.rsqrt(var + EPS) * lg_ref[...] + lb_ref[...]
        o_ref[0, c * ck:(c + 1) * ck, :] = (z * jax.nn.sigmoid(z)).astype(BF16)
    ubuf[0:halo, :] = ubuf[ts:ts + halo, :]


def _conv_module(u, conv_w, conv_b, ln_g, ln_b):
    b, s, cw = u.shape
    ts = TS_CONV
    return pl.pallas_call(
        _conv_kernel,
        grid=(b, s // ts),
        in_specs=[pl.BlockSpec((1, ts, cw), lambda bi, i: (bi, i, 0)),
                  _resident((CONV_KERNEL, cw)), _resident((1, cw)),
                  _resident((1, cw)), _resident((1, cw))],
        out_specs=pl.BlockSpec((1, ts, cw), lambda bi, i: (bi, i, 0)),
        out_shape=jax.ShapeDtypeStruct((b, s, cw), BF16),
        scratch_shapes=[pltpu.VMEM((CONV_HALO + ts, cw), F32),
                        pltpu.VMEM((SUBLANES, ts + CONV_HALO - SUBLANES, cw), F32),
                        pltpu.VMEM((CONV_KERNEL * SUBLANES, cw), F32)],
        compiler_params=_params(2),
        name="conv_module",
    )(u, conv_w, conv_b, ln_g, ln_b)


def _store_row_slabs(ref, val):
    rows = val.shape[0]
    for s in range(ROW_SLAB):
        c = 2 * s * LANES
        lo = lax.bitcast_convert_type(val[:, c:c + LANES].astype(BF16).astype(F32), U32)
        hi = lax.bitcast_convert_type(val[:, c + LANES:c + 2 * LANES].astype(BF16).astype(F32), U32)
        ref[pl.ds(s, rows, stride=ROW_SLAB), :] = hi | (lo >> 16)


def _load_row_slabs(ref, start, rows):
    chunks = []
    for s in range(ROW_SLAB):
        word = ref[pl.ds(start * ROW_SLAB + s, rows, stride=ROW_SLAB), :]
        chunks.append(lax.bitcast_convert_type(word << 16, F32))
        chunks.append(lax.bitcast_convert_type(word & jnp.uint32(0xFFFF0000), F32))
    return chunks


def _outproj_kernel(at_ref, cv_ref, x_ref, wo_ref, g1_ref, a2_ref, s2_ref, wr_ref, br_ref,
                    x1_ref, z_ref, lg_ref):
    mix = (jnp.dot(at_ref[0], wo_ref[0:ATTN_WIDTH, :], preferred_element_type=F32)
           + jnp.dot(cv_ref[0], wo_ref[ATTN_WIDTH:, :], preferred_element_type=F32))
    x1 = x_ref[0] + g1_ref[0] * mix
    x1_ref[0] = x1
    h2 = x1 * lax.rsqrt(jnp.mean(x1 * x1, axis=-1, keepdims=True) + EPS) * a2_ref[0] + s2_ref[0]
    _store_row_slabs(z_ref, h2)
    lg_ref[0] = jnp.dot(h2.astype(BF16), wr_ref[...], preferred_element_type=F32) + br_ref[...]


def _out_proj(attn, conv, x, w_out, g1, a2, s2, w_r, b_r):
    b, s, d = x.shape
    tm = TM_PROJ
    nt = s // tm
    tok = lambda w: pl.BlockSpec((1, tm, w), lambda bi, i: (bi, i, 0))
    per_batch = pl.BlockSpec((1, 1, d), lambda bi, i: (bi, 0, 0))
    return pl.pallas_call(
        _outproj_kernel,
        grid=(b, nt),
        in_specs=[tok(ATTN_WIDTH), tok(CONV_WIDTH), tok(d), _resident((d, d)),
                  per_batch, per_batch, per_batch,
                  _resident((d, LANES)), _resident((1, LANES))],
        out_specs=[tok(d),
                   pl.BlockSpec((tm * ROW_SLAB, LANES), lambda bi, i: (bi * nt + i, 0)),
                   tok(LANES)],
        out_shape=[jax.ShapeDtypeStruct((b, s, d), F32),
                   jax.ShapeDtypeStruct((b * s * ROW_SLAB, LANES), U32),
                   jax.ShapeDtypeStruct((b, s, LANES), F32)],
        compiler_params=_params(2),
        name="out_proj",
    )(attn, conv, x, w_out, g1, a2, s2, w_r, b_r)


def _route_kernel(lg_ref, ri_ref, rr_ref, rw_ref, cnt_ref, carry):
    i = pl.program_id(0)
    tm = TM_ROUTE

    @pl.when(i == 0)
    def _():
        carry[...] = jnp.zeros(carry.shape, F32)

    l = lg_ref[...]
    lane = lax.broadcasted_iota(I32, (tm, LANES), 1).astype(F32)
    vals, idxs = [], []
    for _ in range(TOP_K):
        m = jnp.max(l, axis=-1, keepdims=True)
        ix = jnp.min(jnp.where(l == m, lane, float(LANES)), axis=-1, keepdims=True)
        vals.append(m)
        idxs.append(ix)
        l = jnp.where(lane == ix, -jnp.inf, l)
    es = [jnp.exp(v - vals[0]) for v in vals]
    den = es[0] + es[1] + es[2] + es[3]
    onehot = jnp.zeros((tm, LANES), F32)
    for ix in idxs:
        onehot = onehot + jnp.where(lane == ix, 1.0, 0.0)
    row = lax.broadcasted_iota(I32, (tm, tm), 0)
    col = lax.broadcasted_iota(I32, (tm, tm), 1)
    strict = (col < row).astype(BF16)
    before = jnp.dot(strict, onehot.astype(BF16), preferred_element_type=F32) + carry[0:1, :]
    ri = jnp.zeros((tm, LANES), F32)
    rr = jnp.zeros((tm, LANES), F32)
    rw = jnp.zeros((tm, LANES), F32)
    for k in range(TOP_K):
        rank = jnp.sum(jnp.where(lane == idxs[k], before, 0.0), axis=-1, keepdims=True)
        ri = jnp.where(lane == k, idxs[k], ri)
        rr = jnp.where(lane == k, rank, rr)
        rw = jnp.where(lane == k, es[k] / den, rw)
    ri_ref[...] = ri.astype(I32)
    rr_ref[...] = rr.astype(I32)
    rw_ref[...] = rw
    carry[0:1, :] = carry[0:1, :] + jnp.sum(onehot, axis=0, keepdims=True)
    cnt_ref[...] = carry[...]


def _route(logits):
    t = logits.shape[0]
    tm = TM_ROUTE
    return pl.pallas_call(
        _route_kernel,
        grid=(t // tm,),
        in_specs=[pl.BlockSpec((tm, LANES), lambda i: (i, 0))],
        out_specs=[pl.BlockSpec((tm, LANES), lambda i: (i, 0)),
                   pl.BlockSpec((tm, LANES), lambda i: (i, 0)),
                   pl.BlockSpec((tm, LANES), lambda i: (i, 0)),
                   pl.BlockSpec((SUBLANES, LANES), lambda i: (0, 0))],
        out_shape=[jax.ShapeDtypeStruct((t, LANES), I32),
                   jax.ShapeDtypeStruct((t, LANES), I32),
                   jax.ShapeDtypeStruct((t, LANES), F32),
                   jax.ShapeDtypeStruct((SUBLANES, LANES), F32)],
        scratch_shapes=[pltpu.VMEM((SUBLANES, LANES), F32)],
        compiler_params=_params(1),
        name="route",
    )(logits)


def _issue_row_gather(idx_ref, idx_base, n_rows, src_hbm, dst_buf, sem):
    def body(g, carry):
        for u in range(GATHER_UNROLL):
            r = g * GATHER_UNROLL + u
            t = idx_ref[idx_base + r]
            pltpu.make_async_copy(
                src_hbm.at[pl.ds(pl.multiple_of(t * ROW_SLAB, ROW_SLAB), ROW_SLAB), :],
                dst_buf.at[pl.ds(pl.multiple_of(r * ROW_SLAB, ROW_SLAB), ROW_SLAB), :],
                sem).start(priority=u % 2)
        return carry
    lax.fori_loop(0, n_rows // GATHER_UNROLL, body, 0)


def _wait_row_gather(n_rows, src_hbm, dst_buf, sem):
    pltpu.make_async_copy(src_hbm.at[pl.ds(0, n_rows * ROW_SLAB), :], dst_buf, sem).wait()


def _dispatch_kernel(dest_ref, pad_ref, z_ref, xs_hbm, stage, zeros, sem, zsem):
    i = pl.program_id(0)
    n = pl.num_programs(0)
    tm = TM_DISP
    pad_rows = TM_MOE * ROW_SLAB

    def zero_copy(slot, n_slots):
        start = pl.multiple_of(slot * ROW_SLAB, ROW_SLAB)
        return pltpu.make_async_copy(zeros.at[pl.ds(0, n_slots * ROW_SLAB), :],
                                     xs_hbm.at[pl.ds(start, n_slots * ROW_SLAB), :], zsem)

    def for_pad_pieces(e, fn):
        slot = pad_ref[e]
        n = pad_ref[N_EXPERTS + e]
        size = TM_MOE // 2
        while size >= 1:
            piece = zero_copy(slot, size)
            pl.when((n & size) != 0)(functools.partial(fn, piece))
            slot = slot + (n & size)
            size //= 2

    @pl.when(i == 0)
    def _():
        zeros[...] = jnp.zeros(zeros.shape, U32)
        first_free = pad_ref[2 * N_EXPERTS]
        n_total = xs_hbm.shape[0] // pad_rows

        def start_all(e, carry):
            for_pad_pieces(e, lambda piece: piece.start())
            return carry

        def wait_all(e, carry):
            for_pad_pieces(e, lambda piece: piece.wait())
            return carry

        def start_free(b, carry):
            zero_copy(b * TM_MOE, TM_MOE).start()
            return carry

        def wait_free(b, carry):
            zero_copy(b * TM_MOE, TM_MOE).wait()
            return carry

        lax.fori_loop(0, N_EXPERTS, start_all, 0)
        lax.fori_loop(first_free, n_total, start_free, 0)
        lax.fori_loop(0, N_EXPERTS, wait_all, 0)
        lax.fori_loop(first_free, n_total, wait_free, 0)

    cur = i % 2
    stage[cur] = z_ref[...]

    def body(g, carry):
        for u in range(GATHER_UNROLL // TOP_K):
            r = g * (GATHER_UNROLL // TOP_K) + u
            src = stage.at[cur, pl.ds(pl.multiple_of(r * ROW_SLAB, ROW_SLAB), ROW_SLAB), :]
            for k in range(TOP_K):
                d = dest_ref[(i * tm + r) * TOP_K + k]
                pltpu.make_async_copy(
                    src, xs_hbm.at[pl.ds(pl.multiple_of(d * ROW_SLAB, ROW_SLAB), ROW_SLAB), :],
                    sem.at[cur]).start(priority=k % 2)
        return carry

    lax.fori_loop(0, tm * TOP_K // GATHER_UNROLL, body, 0)

    def drain(slot):
        for _ in range(TOP_K):
            pltpu.make_async_copy(stage.at[slot], xs_hbm.at[pl.ds(0, tm * ROW_SLAB), :], sem.at[slot]).wait()

    @pl.when(i > 0)
    def _():
        drain(1 - cur)

    @pl.when(i == n - 1)
    def _():
        drain(cur)


def _dispatch(dest_flat, pad_start, z, n_slots):
    t = z.shape[0] // ROW_SLAB
    tm = TM_DISP
    return pl.pallas_call(
        _dispatch_kernel,
        grid_spec=pltpu.PrefetchScalarGridSpec(
            num_scalar_prefetch=2,
            grid=(t // tm,),
            in_specs=[pl.BlockSpec((tm * ROW_SLAB, LANES), lambda i, dr, pr: (i, 0))],
            out_specs=pl.BlockSpec(memory_space=pl.ANY),
            scratch_shapes=[pltpu.VMEM((2, tm * ROW_SLAB, LANES), U32),
                            pltpu.VMEM((TM_MOE * ROW_SLAB, LANES), U32),
                            pltpu.SemaphoreType.DMA((2,)), pltpu.SemaphoreType.DMA(())]),
        out_shape=jax.ShapeDtypeStruct((n_slots * ROW_SLAB, LANES), U32),
        compiler_params=_params(1),
        name="dispatch",
    )(dest_flat, pad_start, z)


SCHED_W = 5
S_EXPERT, S_SLOT, S_NEXT, S_C0, S_C1 = range(SCHED_W)


def _expert_weights_step(sched_ref, b, n_used, w_hbm, wbuf, stage, sems, compute):
    n_mat = len(w_hbm)
    n_chunk = wbuf.shape[2] // W_CHUNK

    def rows(rc):
        return pl.ds(pl.multiple_of(rc * W_CHUNK, W_CHUNK), W_CHUNK)

    def copies(e, rc, j):
        return [pltpu.make_async_copy(w_hbm[m].at[e, rows(rc), :], stage.at[j, m], sems.at[j, m])
                for m in range(n_mat)]

    def convert(slot, rc, j):
        for m in range(n_mat):
            wbuf[slot, m, rows(rc), :] = stage[j, m].astype(BF16)

    def load_now(e, slot, rc0, rc1):
        def body(rc, carry):
            for cp in copies(e, rc, 0):
                cp.start()
            for cp in copies(e, rc, 0):
                cp.wait()
            convert(slot, rc, 0)
            return carry
        lax.fori_loop(rc0, rc1, body, 0)

    @pl.when(b == 0)
    def _():
        load_now(sched_ref[S_EXPERT], 0, 0, n_chunk)

    @pl.when(b < n_used)
    def _():
        base = b * SCHED_W
        slot, nxt = sched_ref[base + S_SLOT], sched_ref[base + S_NEXT]
        c0, c1 = sched_ref[base + S_C0], sched_ref[base + S_C1]
        def start_chunk(c, carry):
            for cp in copies(nxt, c, c - c0):
                cp.start()
            return carry
        lax.fori_loop(c0, jnp.minimum(c0 + N_STAGE, c1), start_chunk, 0)
        compute(slot)
        for j in range(N_STAGE):
            @pl.when(c0 + j < c1)
            def _():
                for cp in copies(nxt, c0 + j, j):
                    cp.wait()
                convert(1 - slot, c0 + j, j)
        load_now(nxt, 1 - slot, jnp.minimum(c0 + N_STAGE, c1), c1)


def _moe_up_kernel(sched_ref, nu_ref, xs_ref, wg_hbm, bg_ref, wu_hbm, bu_ref, hid_ref, wbuf, stage, sems):
    b = pl.program_id(0)
    tm = TM_MOE

    def compute(slot):
        x = jnp.concatenate([c.astype(BF16) for c in _load_row_slabs(xs_ref, 0, tm)], axis=-1)
        g = jnp.dot(x, wbuf[slot, 0], preferred_element_type=F32) + bg_ref[0]
        u = jnp.dot(x, wbuf[slot, 1], preferred_element_type=F32) + bu_ref[0]
        g = jnp.minimum(g, SWIGLU_LIMIT)
        u = jnp.clip(u, -SWIGLU_LIMIT, SWIGLU_LIMIT)
        hid_ref[...] = ((u + 1.0) * (g * jax.nn.sigmoid(SWIGLU_ALPHA * g))).astype(BF16)

    _expert_weights_step(sched_ref, b, nu_ref[0], (wg_hbm, wu_hbm), wbuf, stage, sems, compute)

    @pl.when(b >= nu_ref[0])
    def _():
        hid_ref[...] = jnp.zeros(hid_ref.shape, BF16)


def _expert_scratch(n_mat, k, n):
    return [pltpu.VMEM((2, n_mat, k, n), BF16),
            pltpu.VMEM((N_STAGE, n_mat, W_CHUNK, n), F32),
            pltpu.SemaphoreType.DMA((N_STAGE, n_mat))]


def _moe_up(sched, n_used, xs, w_gate, b_gate, w_up, b_up):
    e, d, f = w_gate.shape
    n_blocks = sched.shape[0] // SCHED_W
    tm = TM_MOE
    hbm = pl.BlockSpec(memory_space=pl.ANY)
    bspec = pl.BlockSpec((1, 1, f), lambda b, sc, nu: (sc[b * SCHED_W + S_EXPERT], 0, 0))
    return pl.pallas_call(
        _moe_up_kernel,
        grid_spec=pltpu.PrefetchScalarGridSpec(
            num_scalar_prefetch=2,
            grid=(n_blocks,),
            in_specs=[pl.BlockSpec((tm * ROW_SLAB, LANES), lambda b, sc, nu: (jnp.minimum(b, nu[0] - 1), 0)),
                      hbm, bspec, hbm, bspec],
            out_specs=pl.BlockSpec((tm, f), lambda b, sc, nu: (b, 0)),
            scratch_shapes=_expert_scratch(2, d, f)),
        out_shape=jax.ShapeDtypeStruct((n_blocks * tm, f), BF16),
        compiler_params=_params(1),
        name="moe_up",
    )(sched, n_used, xs, w_gate, b_gate, w_up, b_up)


def _moe_down_kernel(sched_ref, nu_ref, hid_ref, wd_hbm, bd_ref, ys_ref, wbuf, stage, sems):
    b = pl.program_id(0)

    def compute(slot):
        out = jnp.dot(hid_ref[...], wbuf[slot, 0], preferred_element_type=F32) + bd_ref[0]
        _store_row_slabs(ys_ref, out)

    _expert_weights_step(sched_ref, b, nu_ref[0], (wd_hbm,), wbuf, stage, sems, compute)

    @pl.when(b >= nu_ref[0])
    def _():
        ys_ref[...] = jnp.zeros(ys_ref.shape, U32)


def _moe_down(sched, n_used, hid, w_down, b_down):
    e, f, d = w_down.shape
    n_blocks = sched.shape[0] // SCHED_W
    tm = TM_MOE
    return pl.pallas_call(
        _moe_down_kernel,
        grid_spec=pltpu.PrefetchScalarGridSpec(
            num_scalar_prefetch=2,
            grid=(n_blocks,),
            in_specs=[pl.BlockSpec((tm, f), lambda b, sc, nu: (b, 0)),
                      pl.BlockSpec(memory_space=pl.ANY),
                      pl.BlockSpec((1, 1, d), lambda b, sc, nu: (sc[b * SCHED_W + S_EXPERT], 0, 0))],
            out_specs=pl.BlockSpec((tm * ROW_SLAB, LANES), lambda b, sc, nu: (b, 0)),
            scratch_shapes=_expert_scratch(1, f, d)),
        out_shape=jax.ShapeDtypeStruct((n_blocks * tm * ROW_SLAB, LANES), U32),
        compiler_params=_params(1),
        name="moe_down",
    )(sched, n_used, hid, w_down, b_down)


def _combine_kernel(dest_ref, ys_hbm, x1_ref, rw_ref, g2_ref, o_ref, buf, sem):
    i = pl.program_id(0)
    n = pl.num_programs(0)
    tm = TM_COMB
    rows = TOP_K * tm

    @pl.when(i == 0)
    def _():
        _issue_row_gather(dest_ref, 0, rows, ys_hbm, buf.at[0], sem.at[0])

    @pl.when(i + 1 < n)
    def _():
        nxt = (i + 1) % 2
        _issue_row_gather(dest_ref, (i + 1) * rows, rows, ys_hbm, buf.at[nxt], sem.at[nxt])

    cur = i % 2
    _wait_row_gather(rows, ys_hbm, buf.at[cur], sem.at[cur])
    rw = rw_ref[...]
    y = [jnp.zeros((tm, LANES), F32)] * (2 * ROW_SLAB)
    for k in range(TOP_K):
        w_k = rw[:, k:k + 1]
        y = [yc + w_k * c for yc, c in zip(y, _load_row_slabs(buf.at[cur], k * tm, tm))]
    o_ref[...] = x1_ref[...] + g2_ref[0] * jnp.concatenate(y, axis=-1)


def _combine(dest_km, ys, x1, rw, g2, seq):
    t, d = x1.shape
    tm = TM_COMB
    per_seq = seq // tm
    return pl.pallas_call(
        _combine_kernel,
        grid_spec=pltpu.PrefetchScalarGridSpec(
            num_scalar_prefetch=1,
            grid=(t // tm,),
            in_specs=[pl.BlockSpec(memory_space=pl.ANY),
                      pl.BlockSpec((tm, d), lambda i, dr: (i, 0)),
                      pl.BlockSpec((tm, LANES), lambda i, dr: (i, 0)),
                      pl.BlockSpec((1, 1, d), lambda i, dr: (i // per_seq, 0, 0))],
            out_specs=pl.BlockSpec((tm, d), lambda i, dr: (i, 0)),
            scratch_shapes=[pltpu.VMEM((2, TOP_K * tm * ROW_SLAB, LANES), U32),
                            pltpu.SemaphoreType.DMA((2,))]),
        out_shape=jax.ShapeDtypeStruct((t, d), F32),
        compiler_params=_params(1),
        name="combine",
    )(dest_km, ys, x1, rw, g2)


def _pad_cols(w, n):
    return jnp.pad(w, ((0, 0), (0, n - w.shape[1])))


def _layer(x, mod, norm_mix_g, norm_ffn_g, w_in, b_f, q_norm_g, k_norm_g, conv_w, conv_b,
           conv_ln_g, conv_ln_b, w_out, w_router, b_router, w_gate, b_gate, w_up, b_up,
           w_down, b_down):
    b, s, d = x.shape
    t = b * s
    shift1, scale1, gate1, shift2, scale2, gate2 = [m[:, None, :] for m in jnp.split(mod, 6, axis=-1)]
    a1 = norm_mix_g[None, None, :] * (1.0 + scale1)
    a2 = norm_ffn_g[None, None, :] * (1.0 + scale2)

    aw = ATTN_WIDTH
    w_qkv = w_in[:, :3 * aw].astype(BF16)
    w_conv = w_in[:, 3 * aw + ATTN_HEADS:].astype(BF16)
    w_f = _pad_cols(w_in[:, 3 * aw:3 * aw + ATTN_HEADS], LANES).astype(BF16)
    b_f_pad = _pad_cols(b_f[None, :], LANES)

    q, k, v, u, logf = _in_proj(x, a1, shift1, w_qkv, w_conv, w_f, b_f_pad,
                                q_norm_g[None, :], k_norm_g[None, :])
    attn = _fox_attention(q, k, v, _forget_cumsum(logf))
    conv = _conv_module(u, conv_w, conv_b[None, :], conv_ln_g[None, :], conv_ln_b[None, :])

    w_r = _pad_cols(w_router, LANES).astype(BF16)
    b_r = jnp.concatenate([b_router, jnp.full((LANES - N_EXPERTS,), NEG_BIG, F32)])[None, :]
    x1, z, logits = _out_proj(attn, conv, x, w_out.astype(BF16), gate1, a2, shift2, w_r, b_r)

    ri, rr, rw, cnt = _route(logits.reshape(t, LANES))

    tm = TM_MOE
    n_blocks = t * TOP_K // tm + N_EXPERTS
    counts = cnt[0, :N_EXPERTS].astype(I32)
    padded = (counts + tm - 1) // tm * tm
    padded_end = jnp.cumsum(padded)
    padded_start = padded_end - padded
    start_of = jnp.zeros_like(ri)
    for e in range(N_EXPERTS):
        start_of = jnp.where(ri == e, padded_start[e], start_of)
    dest = (start_of + rr)[:, :TOP_K]
    block_start = jnp.arange(n_blocks, dtype=I32)[:, None] * tm
    block_e = jnp.minimum(jnp.sum((padded_end[None, :] <= block_start).astype(I32), axis=1), N_EXPERTS - 1)
    n_used = (padded_end[-1:] // tm).astype(I32)
    dest_km = dest.reshape(t // TM_COMB, TM_COMB, TOP_K).transpose(0, 2, 1).reshape(-1)

    pad_table = jnp.concatenate([padded_start + counts, padded - counts, n_used])
    xs = _dispatch(dest.reshape(-1), pad_table, z, n_blocks * tm)

    experts = jnp.arange(N_EXPERTS, dtype=I32)
    has = counts > 0
    later = (experts[None, :] > experts[:, None]) & has[None, :]
    next_e = jnp.min(jnp.where(later, experts[None, :], N_EXPERTS), axis=1)
    run_slot = (jnp.cumsum(has.astype(I32)) - 1) % 2
    of_block = block_e[:, None] == experts[None, :]

    def per_block(table):
        return jnp.sum(jnp.where(of_block, table[None, :], 0), axis=1)

    run_len = per_block(jnp.maximum(padded // tm, 1))
    pos = jnp.arange(n_blocks, dtype=I32) - per_block(padded_start // tm)
    nxt = per_block(next_e)
    live = nxt < N_EXPERTS
    c0 = jnp.where(live, N_WCHUNK * pos // run_len, 0)
    c1 = jnp.where(live, N_WCHUNK * (pos + 1) // run_len, 0)
    sched = jnp.stack([block_e, per_block(run_slot), jnp.where(live, nxt, 0), c0, c1], axis=1).reshape(-1)

    hid = _moe_up(sched, n_used, xs, w_gate, b_gate[:, None, :], w_up, b_up[:, None, :])
    ys = _moe_down(sched, n_used, hid, w_down, b_down[:, None, :])
    out = _combine(dest_km, ys, x1.reshape(t, d), rw, gate2, s)
    return out.reshape(b, s, d)


def kernel(x, c, ada_w, ada_b, norm_mix_g, norm_ffn_g, w_in, b_f, q_norm_g, k_norm_g, conv_w, conv_b,
           conv_ln_g, conv_ln_b, w_out, w_router, b_router, w_gate, b_gate, w_up, b_up, w_down, b_down):
    b = x.shape[0]
    c_pad = jnp.pad(c, ((0, SUBLANES - b), (0, 0)))
    for l in range(ada_w.shape[0]):
        mod = _ada_mod(c_pad, ada_w[l], ada_b[l])[:b]
        x = _layer(x, mod, norm_mix_g[l], norm_ffn_g[l], w_in[l], b_f[l], q_norm_g[l], k_norm_g[l],
                   conv_w[l], conv_b[l], conv_ln_g[l], conv_ln_b[l], w_out[l], w_router[l],
                   b_router[l], w_gate[l], b_gate[l], w_up[l], b_up[l], w_down[l], b_down[l])
    return x
```

```python
import functools

import jax
import jax.numpy as jnp
from jax import lax
from jax.experimental import pallas as pl
from jax.experimental.pallas import tpu as pltpu

F32 = jnp.float32
BF16 = jnp.bfloat16
I32 = jnp.int32
U32 = jnp.uint32

D_MODEL = 2048
ATTN_HEADS = 8
HEAD_DIM = 128
ATTN_WIDTH = ATTN_HEADS * HEAD_DIM
CONV_WIDTH = D_MODEL - ATTN_WIDTH
CONV_KERNEL = 31
N_EXPERTS = 32
TOP_K = 4
SWIGLU_LIMIT = 7.0
SWIGLU_ALPHA = 1.702
EPS = 1e-6
LOG2E = 1.4426950408889634

LANES = 128
SUBLANES = 8
HALF = D_MODEL // 2
ROW_SLAB = HALF // LANES
VMEM_LIMIT = 56 * 1024 * 1024

TM_PROJ = 512
TQ = 512
ATTN_V_ROWS = HEAD_DIM + 16
HEADS_PER_STEP = 2
CUM_CHUNK = 256
TS_CONV = 256
CONV_HALO = 32
CONV_CHUNK = 32
TM_ROUTE = 512
TM_MOE = 256
TM_DISP = 256
TM_COMB = 256
W_CHUNK = 256
N_WCHUNK = D_MODEL // W_CHUNK
N_STAGE = 2
NEG_BIG = -1e30


def _params(n_axes):
    return pltpu.CompilerParams(
        dimension_semantics=("arbitrary",) * n_axes, vmem_limit_bytes=VMEM_LIMIT)


def _resident(shape):
    nd = len(shape)
    return pl.BlockSpec(shape, lambda *_: (0,) * nd, pipeline_mode=pl.Buffered(1))


def _ada_kernel(c_ref, w_ref, b_ref, o_ref):
    c = c_ref[...]
    c_act = (c * jax.nn.sigmoid(c)).astype(BF16)
    o_ref[...] = jnp.dot(c_act, w_ref[...].astype(BF16), preferred_element_type=F32) + b_ref[...]


def _ada_mod(c_pad, ada_w, ada_b):
    rows, d = c_pad.shape
    n = ada_w.shape[1]
    tn = 1024
    return pl.pallas_call(
        _ada_kernel,
        grid=(n // tn,),
        in_specs=[pl.BlockSpec((rows, d), lambda j: (0, 0)),
                  pl.BlockSpec((d, tn), lambda j: (0, j)),
                  pl.BlockSpec((1, tn), lambda j: (0, j))],
        out_specs=pl.BlockSpec((rows, tn), lambda j: (0, j)),
        out_shape=jax.ShapeDtypeStruct((rows, n), F32),
        compiler_params=_params(1),
        name="ada_mod",
    )(c_pad, ada_w, ada_b.reshape(1, n))


def _log_sigmoid(x):
    return jnp.minimum(x, 0.0) - jnp.log1p(jnp.exp(-jnp.abs(x)))


def _head_rms(y, g):
    outs = []
    for h in range(ATTN_HEADS):
        yh = y[:, h * HEAD_DIM:(h + 1) * HEAD_DIM]
        r = lax.rsqrt(jnp.mean(yh * yh, axis=-1, keepdims=True) + EPS)
        outs.append(yh * r * g)
    return jnp.concatenate(outs, axis=-1)


def _inproj_kernel(x_ref, a_ref, s_ref, w_ref, wc_ref, wf_ref, bf_ref, qg_ref, kg_ref,
                   q_ref, k_ref, v_ref, u_ref, f_ref):
    x = x_ref[0]
    h = x * lax.rsqrt(jnp.mean(x * x, axis=-1, keepdims=True) + EPS) * a_ref[0] + s_ref[0]
    hb = h.astype(BF16)
    aw = ATTN_WIDTH
    q = jnp.dot(hb, w_ref[:, 0:aw], preferred_element_type=F32)
    q_ref[0] = (_head_rms(q, qg_ref[...]) * (LOG2E * HEAD_DIM ** -0.5)).astype(BF16)
    k = jnp.dot(hb, w_ref[:, aw:2 * aw], preferred_element_type=F32)
    k_ref[0] = _head_rms(k, kg_ref[...]).astype(BF16)
    v_ref[0] = jnp.dot(hb, w_ref[:, 2 * aw:3 * aw], preferred_element_type=F32).astype(BF16)
    a = jnp.dot(hb, wc_ref[:, 0:CONV_WIDTH], preferred_element_type=F32)
    g = jnp.dot(hb, wc_ref[:, CONV_WIDTH:], preferred_element_type=F32)
    u_ref[0] = (a * jax.nn.sigmoid(g)).astype(BF16)
    fl = jnp.dot(hb, wf_ref[...], preferred_element_type=F32) + bf_ref[...]
    f_ref[0] = _log_sigmoid(fl)


def _in_proj(x, a1, s1, w_qkv, w_conv, w_f, b_f, q_g, k_g):
    b, s, d = x.shape
    tm = TM_PROJ
    tok = lambda w: pl.BlockSpec((1, tm, w), lambda bi, i: (bi, i, 0))
    per_batch = pl.BlockSpec((1, 1, d), lambda bi, i: (bi, 0, 0))
    return pl.pallas_call(
        _inproj_kernel,
        grid=(b, s // tm),
        in_specs=[tok(d), per_batch, per_batch,
                  _resident(w_qkv.shape), _resident(w_conv.shape), _resident((d, LANES)), _resident((1, LANES)),
                  _resident((1, HEAD_DIM)), _resident((1, HEAD_DIM))],
        out_specs=[tok(ATTN_WIDTH), tok(ATTN_WIDTH), tok(ATTN_WIDTH), tok(CONV_WIDTH), tok(LANES)],
        out_shape=[jax.ShapeDtypeStruct((b, s, ATTN_WIDTH), BF16)] * 3
        + [jax.ShapeDtypeStruct((b, s, CONV_WIDTH), BF16),
           jax.ShapeDtypeStruct((b, s, LANES), F32)],
        compiler_params=_params(2),
        name="in_proj",
    )(x, a1, s1, w_qkv, w_conv, w_f, b_f, q_g, k_g)


def _bf16_pieces(c):
    p0 = c.astype(BF16)
    r0 = c - p0.astype(F32)
    p1 = r0.astype(BF16)
    p2 = (r0 - p1.astype(F32)).astype(BF16)
    return p0, p1, p2


def _cumsum_kernel(f_ref, c_ref):
    ch = CUM_CHUNK
    s = f_ref.shape[1]
    row = lax.broadcasted_iota(I32, (ch, ch), 0)
    col = lax.broadcasted_iota(I32, (ch, ch), 1)
    tri = (col <= row).astype(BF16)
    carry = jnp.zeros((1, LANES), F32)
    for i in range(s // ch):
        cs = carry
        for p in _bf16_pieces(f_ref[0, i * ch:(i + 1) * ch, :]):
            cs = cs + jnp.dot(tri, p, preferred_element_type=F32)
        c_ref[0, i * ch:(i + 1) * ch, :] = cs
        carry = cs[ch - 1:ch, :]


def _forget_cumsum(logf):
    b, s, _ = logf.shape
    return pl.pallas_call(
        _cumsum_kernel,
        grid=(b,),
        in_specs=[pl.BlockSpec((1, s, LANES), lambda bi: (bi, 0, 0))],
        out_specs=pl.BlockSpec((1, s, LANES), lambda bi: (bi, 0, 0)),
        out_shape=jax.ShapeDtypeStruct((b, s, LANES), F32),
        compiler_params=_params(1),
        name="forget_cumsum",
    )(logf)


def _attn_kernel(q_ref, k_ref, v_ref, c_ref, o_ref, kx_sc, qx_sc, vt_sc, s_sc, m_sc, acc_sc):
    tq = TQ
    n_t = k_ref.shape[1] // tq
    heads = range(HEADS_PER_STEP)
    sel_r = lax.broadcasted_iota(I32, (LANES, LANES), 0)
    sel_c = lax.broadcasted_iota(I32, (LANES, LANES), 1)
    lane1 = lax.broadcasted_iota(I32, (1, LANES), 1)

    def head_lanes(hh):
        return slice(hh * HEAD_DIM, (hh + 1) * HEAD_DIM)

    def spread(pieces, hh, first_lane, sign):
        head = pl.program_id(1) * HEADS_PER_STEP + hh
        out = jnp.zeros(pieces[0].shape, F32)
        for i, p in enumerate(pieces):
            sel = jnp.where((sel_r == head) & (sel_c == first_lane + i), sign, 0.0).astype(BF16)
            out = out + jnp.dot(p, sel, preferred_element_type=F32)
        return out

    ones_k = jnp.where((lane1 >= 3) & (lane1 < 6), 1.0, 0.0)
    ones_q = jnp.where(lane1 < 3, 1.0, 0.0)
    extra = lax.broadcasted_iota(I32, (ATTN_V_ROWS - HEAD_DIM, tq), 0)
    ones_row = jnp.where(extra == 0, 1.0, 0.0).astype(BF16)

    def fill(i, carry):
        st = pl.multiple_of(i * tq, tq)
        pieces = _bf16_pieces(c_ref[0, pl.ds(st, tq), :] * LOG2E)
        for hh in heads:
            kx_sc[hh, pl.ds(st, tq), :] = (spread(pieces, hh, 0, -1.0) + ones_k).astype(BF16)
            qx_sc[hh, pl.ds(st, tq), :] = (spread(pieces, hh, 3, 1.0) + ones_q).astype(BF16)
            v = v_ref[0, pl.ds(st, tq), head_lanes(hh)]
            vt_sc[hh, 0:HEAD_DIM, pl.ds(st, tq)] = v.astype(F32).T.astype(BF16)
            vt_sc[hh, HEAD_DIM:, pl.ds(st, tq)] = ones_row
        return carry

    lax.fori_loop(0, n_t, fill, 0)

    def scores_into(i, j, slot):
        sq = pl.multiple_of(i * tq, tq)
        sk = pl.multiple_of(j * tq, tq)
        for hh in heads:
            q_aug = jnp.concatenate(
                [q_ref[0, pl.ds(sq, tq), head_lanes(hh)], qx_sc[hh, pl.ds(sq, tq), :]], axis=-1)
            k_aug = jnp.concatenate(
                [k_ref[0, pl.ds(sk, tq), head_lanes(hh)], kx_sc[hh, pl.ds(sk, tq), :]], axis=-1)
            s_sc[slot, hh] = lax.dot_general(k_aug, q_aug, (((1,), (1,)), ((), ())),
                                             preferred_element_type=F32)

    def reset():
        m_sc[...] = jnp.full(m_sc.shape, -jnp.inf, F32)
        acc_sc[...] = jnp.zeros(acc_sc.shape, F32)

    def update(j, slot, masked):
        st = pl.multiple_of(j * tq, tq)
        for hh in heads:
            s = s_sc[slot, hh]
            if masked:
                key = lax.broadcasted_iota(I32, (tq, tq), 0)
                qry = lax.broadcasted_iota(I32, (tq, tq), 1)
                s = jnp.where(key <= qry, s, -jnp.inf)
            m_prev = m_sc[hh]
            m_new = jnp.maximum(m_prev, jnp.max(s, axis=0, keepdims=True))
            alpha = jnp.exp2(m_prev - m_new)
            p = jnp.exp2(s - m_new)
            pv = jnp.dot(vt_sc[hh, :, pl.ds(st, tq)], p.astype(BF16), preferred_element_type=F32)
            acc_sc[hh] = alpha * acc_sc[hh] + pv
            m_sc[hh] = m_new

    def finalize(i):
        sq = pl.multiple_of(i * tq, tq)
        for hh in heads:
            acc = acc_sc[hh]
            o_ref[0, pl.ds(sq, tq), head_lanes(hh)] = (
                acc[0:HEAD_DIM] / acc[HEAD_DIM:HEAD_DIM + 1]).T.astype(BF16)
        reset()

    def tile(i, j, slot):
        diag = j == i
        ni = jnp.where(diag, i + 1, i)
        nj = jnp.where(diag, 0, j + 1)
        si = jnp.minimum(ni, n_t - 1)
        sj = jnp.where(ni == n_t, n_t - 1, nj)

        @pl.when(diag)
        def _():
            scores_into(si, sj, 1 - slot)
            update(j, slot, True)
            finalize(i)

        @pl.when(jnp.logical_not(diag))
        def _():
            scores_into(si, sj, 1 - slot)
            update(j, slot, False)

        return ni, nj

    reset()
    scores_into(0, 0, 0)

    def body(t, ij):
        ij = tile(ij[0], ij[1], 0)
        return tile(ij[0], ij[1], 1)

    n_tiles = n_t * (n_t + 1) // 2
    assert n_tiles % 2 == 0
    lax.fori_loop(0, n_tiles // 2, body, (jnp.int32(0), jnp.int32(0)))


def _fox_attention(q, k, v, cum):
    b, s, _ = q.shape
    tq, hp = TQ, HEADS_PER_STEP
    head = pl.BlockSpec((1, s, hp * HEAD_DIM), lambda bi, g: (bi, 0, g))
    return pl.pallas_call(
        _attn_kernel,
        grid=(b, ATTN_HEADS // hp),
        in_specs=[head, head, head, pl.BlockSpec((1, s, LANES), lambda bi, g: (bi, 0, 0))],
        out_specs=head,
        out_shape=jax.ShapeDtypeStruct((b, s, ATTN_WIDTH), BF16),
        scratch_shapes=[pltpu.VMEM((hp, s, LANES), BF16),
                        pltpu.VMEM((hp, s, LANES), BF16),
                        pltpu.VMEM((hp, ATTN_V_ROWS, s), BF16),
                        pltpu.VMEM((2, hp, tq, tq), F32),
                        pltpu.VMEM((hp, 1, tq), F32),
                        pltpu.VMEM((hp, ATTN_V_ROWS, tq), F32)],
        compiler_params=_params(2),
        name="fox_attention",
    )(q, k, v, cum)


def _conv_kernel(u_ref, w_ref, cb_ref, lg_ref, lb_ref, o_ref, ubuf, shifted, wb):
    i = pl.program_id(1)
    ts, halo, ck = TS_CONV, CONV_HALO, CONV_CHUNK

    @pl.when(i == 0)
    def _():
        ubuf[0:halo, :] = jnp.zeros((halo, CONV_WIDTH), F32)
        for j in range(CONV_KERNEL):
            wb[j * SUBLANES:(j + 1) * SUBLANES, :] = jnp.broadcast_to(
                w_ref[j:j + 1, :], (SUBLANES, CONV_WIDTH))

    ubuf[halo:halo + ts, :] = u_ref[0].astype(F32)
    base = halo - (CONV_KERNEL - 1)
    span = shifted.shape[1]
    for r in range(1, SUBLANES):
        shifted[r] = ubuf[r:r + span, :]
    for c in range(ts // ck):
        acc = jnp.zeros((ck, CONV_WIDTH), F32)
        for j in range(CONV_KERNEL):
            off = c * ck + base + j
            r, al = off % SUBLANES, off - off % SUBLANES
            slab = ubuf[al:al + ck, :] if r == 0 else shifted[r, al:al + ck, :]
            wj = wb[j * SUBLANES:(j + 1) * SUBLANES, :]
            acc = acc + slab * jnp.concatenate([wj] * (ck // SUBLANES), axis=0)
        y = acc + cb_ref[...]
        mu = jnp.mean(y, axis=-1, keepdims=True)
        yc = y - mu
        var = jnp.mean(yc * yc, axis=-1, keepdims=True)
        z = yc * lax.rsqrt(var + EPS) * lg_ref[...] + lb_ref[...]
        o_ref[0, c * ck:(c + 1) * ck, :] = (z * jax.nn.sigmoid(z)).astype(BF16)
    ubuf[0:halo, :] = ubuf[ts:ts + halo, :]


def _conv_module(u, conv_w, conv_b, ln_g, ln_b):
    b, s, cw = u.shape
    ts = TS_CONV
    return pl.pallas_call(
        _conv_kernel,
        grid=(b, s // ts),
        in_specs=[pl.BlockSpec((1, ts, cw), lambda bi, i: (bi, i, 0)),
                  _resident((CONV_KERNEL, cw)), _resident((1, cw)),
                  _resident((1, cw)), _resident((1, cw))],
        out_specs=pl.BlockSpec((1, ts, cw), lambda bi, i: (bi, i, 0)),
        out_shape=jax.ShapeDtypeStruct((b, s, cw), BF16),
        scratch_shapes=[pltpu.VMEM((CONV_HALO + ts, cw), F32),
                        pltpu.VMEM((SUBLANES, ts + CONV_HALO - SUBLANES, cw), F32),
                        pltpu.VMEM((CONV_KERNEL * SUBLANES, cw), F32)],
        compiler_params=_params(2),
        name="conv_module",
    )(u, conv_w, conv_b, ln_g, ln_b)


def _store_row_slabs(ref, val):
    rows = val.shape[0]
    for s in range(ROW_SLAB):
        c = 2 * s * LANES
        lo = lax.bitcast_convert_type(val[:, c:c + LANES].astype(BF16).astype(F32), U32)
        hi = lax.bitcast_convert_type(val[:, c + LANES:c + 2 * LANES].astype(BF16).astype(F32), U32)
        ref[pl.ds(s, rows, stride=ROW_SLAB), :] = hi | (lo >> 16)


def _load_row_slabs(ref, start, rows):
    chunks = []
    for s in range(ROW_SLAB):
        word = ref[pl.ds(start * ROW_SLAB + s, rows, stride=ROW_SLAB), :]
        chunks.append(lax.bitcast_convert_type(word << 16, F32))
        chunks.append(lax.bitcast_convert_type(word & jnp.uint32(0xFFFF0000), F32))
    return chunks


def _outproj_kernel(at_ref, cv_ref, x_ref, wo_ref, g1_ref, a2_ref, s2_ref, wr_ref, br_ref,
                    x1_ref, z_ref, lg_ref):
    mix = (jnp.dot(at_ref[0], wo_ref[0:ATTN_WIDTH, :], preferred_element_type=F32)
           + jnp.dot(cv_ref[0], wo_ref[ATTN_WIDTH:, :], preferred_element_type=F32))
    x1 = x_ref[0] + g1_ref[0] * mix
    x1_ref[0] = x1
    h2 = x1 * lax.rsqrt(jnp.mean(x1 * x1, axis=-1, keepdims=True) + EPS) * a2_ref[0] + s2_ref[0]
    _store_row_slabs(z_ref, h2)
    lg_ref[0] = jnp.dot(h2.astype(BF16), wr_ref[...], preferred_element_type=F32) + br_ref[...]


def _out_proj(attn, conv, x, w_out, g1, a2, s2, w_r, b_r):
    b, s, d = x.shape
    tm = TM_PROJ
    nt = s // tm
    tok = lambda w: pl.BlockSpec((1, tm, w), lambda bi, i: (bi, i, 0))
    per_batch = pl.BlockSpec((1, 1, d), lambda bi, i: (bi, 0, 0))
    return pl.pallas_call(
        _outproj_kernel,
        grid=(b, nt),
        in_specs=[tok(ATTN_WIDTH), tok(CONV_WIDTH), tok(d), _resident((d, d)),
                  per_batch, per_batch, per_batch,
                  _resident((d, LANES)), _resident((1, LANES))],
        out_specs=[tok(d),
                   pl.BlockSpec((tm * ROW_SLAB, LANES), lambda bi, i: (bi * nt + i, 0)),
                   tok(LANES)],
        out_shape=[jax.ShapeDtypeStruct((b, s, d), F32),
                   jax.ShapeDtypeStruct((b * s * ROW_SLAB, LANES), U32),
                   jax.ShapeDtypeStruct((b, s, LANES), F32)],
        compiler_params=_params(2),
        name="out_proj",
    )(attn, conv, x, w_out, g1, a2, s2, w_r, b_r)


def _route_kernel(lg_ref, ri_ref, rr_ref, rw_ref, cnt_ref, carry):
    i = pl.program_id(0)
    tm = TM_ROUTE

    @pl.when(i == 0)
    def _():
        carry[...] = jnp.zeros(carry.shape, F32)

    l = lg_ref[...]
    lane = lax.broadcasted_iota(I32, (tm, LANES), 1).astype(F32)
    vals, idxs = [], []
    for _ in range(TOP_K):
        m = jnp.max(l, axis=-1, keepdims=True)
        ix = jnp.min(jnp.where(l == m, lane, float(LANES)), axis=-1, keepdims=True)
        vals.append(m)
        idxs.append(ix)
        l = jnp.where(lane == ix, -jnp.inf, l)
    es = [jnp.exp(v - vals[0]) for v in vals]
    den = es[0] + es[1] + es[2] + es[3]
    onehot = jnp.zeros((tm, LANES), F32)
    for ix in idxs:
        onehot = onehot + jnp.where(lane == ix, 1.0, 0.0)
    row = lax.broadcasted_iota(I32, (tm, tm), 0)
    col = lax.broadcasted_iota(I32, (tm, tm), 1)
    strict = (col < row).astype(BF16)
    before = jnp.dot(strict, onehot.astype(BF16), preferred_element_type=F32) + carry[0:1, :]
    ri = jnp.zeros((tm, LANES), F32)
    rr = jnp.zeros((tm, LANES), F32)
    rw = jnp.zeros((tm, LANES), F32)
    for k in range(TOP_K):
        rank = jnp.sum(jnp.where(lane == idxs[k], before, 0.0), axis=-1, keepdims=True)
        ri = jnp.where(lane == k, idxs[k], ri)
        rr = jnp.where(lane == k, rank, rr)
        rw = jnp.where(lane == k, es[k] / den, rw)
    ri_ref[...] = ri.astype(I32)
    rr_ref[...] = rr.astype(I32)
    rw_ref[...] = rw
    carry[0:1, :] = carry[0:1, :] + jnp.sum(onehot, axis=0, keepdims=True)
    cnt_ref[...] = carry[...]


def _route(logits):
    t = logits.shape[0]
    tm = TM_ROUTE
    return pl.pallas_call(
        _route_kernel,
        grid=(t // tm,),
        in_specs=[pl.BlockSpec((tm, LANES), lambda i: (i, 0))],
        out_specs=[pl.BlockSpec((tm, LANES), lambda i: (i, 0)),
                   pl.BlockSpec((tm, LANES), lambda i: (i, 0)),
                   pl.BlockSpec((tm, LANES), lambda i: (i, 0)),
                   pl.BlockSpec((SUBLANES, LANES), lambda i: (0, 0))],
        out_shape=[jax.ShapeDtypeStruct((t, LANES), I32),
                   jax.ShapeDtypeStruct((t, LANES), I32),
                   jax.ShapeDtypeStruct((t, LANES), F32),
                   jax.ShapeDtypeStruct((SUBLANES, LANES), F32)],
        scratch_shapes=[pltpu.VMEM((SUBLANES, LANES), F32)],
        compiler_params=_params(1),
        name="route",
    )(logits)


def _issue_row_gather(idx_ref, idx_base, n_rows, src_hbm, dst_buf, sem):
    for r in range(n_rows):
        t = idx_ref[idx_base + r]
        pltpu.make_async_copy(
            src_hbm.at[pl.ds(pl.multiple_of(t * ROW_SLAB, ROW_SLAB), ROW_SLAB), :],
            dst_buf.at[pl.ds(r * ROW_SLAB, ROW_SLAB), :],
            sem).start(priority=r % 2)


def _wait_row_gather(n_rows, src_hbm, dst_buf, sem):
    pltpu.make_async_copy(src_hbm.at[pl.ds(0, n_rows * ROW_SLAB), :], dst_buf, sem).wait()


def _dispatch_kernel(dest_ref, pad_ref, z_ref, xs_hbm, stage, zeros, sem, zsem):
    i = pl.program_id(0)
    n = pl.num_programs(0)
    tm = TM_DISP
    pad_rows = TM_MOE * ROW_SLAB

    def zero_copy(slot, n_slots):
        start = pl.multiple_of(slot * ROW_SLAB, ROW_SLAB)
        return pltpu.make_async_copy(zeros.at[pl.ds(0, n_slots * ROW_SLAB), :],
                                     xs_hbm.at[pl.ds(start, n_slots * ROW_SLAB), :], zsem)

    def for_pad_pieces(e, fn):
        slot = pad_ref[e]
        n = pad_ref[N_EXPERTS + e]
        size = TM_MOE // 2
        while size >= 1:
            piece = zero_copy(slot, size)
            pl.when((n & size) != 0)(functools.partial(fn, piece))
            slot = slot + (n & size)
            size //= 2

    @pl.when(i == 0)
    def _():
        zeros[...] = jnp.zeros(zeros.shape, U32)
        first_free = pad_ref[2 * N_EXPERTS]
        n_total = xs_hbm.shape[0] // pad_rows

        def start_all(e, carry):
            for_pad_pieces(e, lambda piece: piece.start())
            return carry

        def wait_all(e, carry):
            for_pad_pieces(e, lambda piece: piece.wait())
            return carry

        def start_free(b, carry):
            zero_copy(b * TM_MOE, TM_MOE).start()
            return carry

        def wait_free(b, carry):
            zero_copy(b * TM_MOE, TM_MOE).wait()
            return carry

        lax.fori_loop(0, N_EXPERTS, start_all, 0)
        lax.fori_loop(first_free, n_total, start_free, 0)
        lax.fori_loop(0, N_EXPERTS, wait_all, 0)
        lax.fori_loop(first_free, n_total, wait_free, 0)

    cur = i % 2
    for par in range(2):
        @pl.when(cur == par)
        def _():
            stage[par] = z_ref[...]
            for r in range(tm):
                src = stage.at[par, pl.ds(r * ROW_SLAB, ROW_SLAB), :]
                for k in range(TOP_K):
                    d = dest_ref[(i * tm + r) * TOP_K + k]
                    pltpu.make_async_copy(
                        src, xs_hbm.at[pl.ds(pl.multiple_of(d * ROW_SLAB, ROW_SLAB), ROW_SLAB), :],
                        sem.at[par]).start(priority=k % 2)

    def drain(slot):
        for _ in range(TOP_K):
            pltpu.make_async_copy(stage.at[slot], xs_hbm.at[pl.ds(0, tm * ROW_SLAB), :], sem.at[slot]).wait()

    @pl.when(i > 0)
    def _():
        drain(1 - cur)

    @pl.when(i == n - 1)
    def _():
        drain(cur)


def _dispatch(dest_flat, pad_start, z, n_slots):
    t = z.shape[0] // ROW_SLAB
    tm = TM_DISP
    return pl.pallas_call(
        _dispatch_kernel,
        grid_spec=pltpu.PrefetchScalarGridSpec(
            num_scalar_prefetch=2,
            grid=(t // tm,),
            in_specs=[pl.BlockSpec((tm * ROW_SLAB, LANES), lambda i, dr, pr: (i, 0))],
            out_specs=pl.BlockSpec(memory_space=pl.ANY),
            scratch_shapes=[pltpu.VMEM((2, tm * ROW_SLAB, LANES), U32),
                            pltpu.VMEM((TM_MOE * ROW_SLAB, LANES), U32),
                            pltpu.SemaphoreType.DMA((2,)), pltpu.SemaphoreType.DMA(())]),
        out_shape=jax.ShapeDtypeStruct((n_slots * ROW_SLAB, LANES), U32),
        compiler_params=_params(1),
        name="dispatch",
    )(dest_flat, pad_start, z)


SCHED_W = 5
S_EXPERT, S_SLOT, S_NEXT, S_C0, S_C1 = range(SCHED_W)


def _expert_weights_step(sched_ref, b, n_used, w_hbm, wbuf, stage, sems, compute):
    n_mat = len(w_hbm)
    n_chunk = wbuf.shape[2] // W_CHUNK

    def rows(rc):
        return pl.ds(pl.multiple_of(rc * W_CHUNK, W_CHUNK), W_CHUNK)

    def copies(e, rc, j):
        return [pltpu.make_async_copy(w_hbm[m].at[e, rows(rc), :], stage.at[j, m], sems.at[j, m])
                for m in range(n_mat)]

    def convert(slot, rc, j):
        for m in range(n_mat):
            wbuf[slot, m, rows(rc), :] = stage[j, m].astype(BF16)

    def load_now(e, slot, rc0, rc1):
        def body(rc, carry):
            for cp in copies(e, rc, 0):
                cp.start()
            for cp in copies(e, rc, 0):
                cp.wait()
            convert(slot, rc, 0)
            return carry
        lax.fori_loop(rc0, rc1, body, 0)

    @pl.when(b == 0)
    def _():
        load_now(sched_ref[S_EXPERT], 0, 0, n_chunk)

    @pl.when(b < n_used)
    def _():
        base = b * SCHED_W
        slot, nxt = sched_ref[base + S_SLOT], sched_ref[base + S_NEXT]
        c0, c1 = sched_ref[base + S_C0], sched_ref[base + S_C1]
        def start_chunk(c, carry):
            for cp in copies(nxt, c, c - c0):
                cp.start()
            return carry
        lax.fori_loop(c0, jnp.minimum(c0 + N_STAGE, c1), start_chunk, 0)
        compute(slot)
        for j in range(N_STAGE):
            @pl.when(c0 + j < c1)
            def _():
                for cp in copies(nxt, c0 + j, j):
                    cp.wait()
                convert(1 - slot, c0 + j, j)
        load_now(nxt, 1 - slot, jnp.minimum(c0 + N_STAGE, c1), c1)


def _moe_up_kernel(sched_ref, nu_ref, xs_ref, wg_hbm, bg_ref, wu_hbm, bu_ref, hid_ref, wbuf, stage, sems):
    b = pl.program_id(0)
    tm = TM_MOE

    def compute(slot):
        x = jnp.concatenate([c.astype(BF16) for c in _load_row_slabs(xs_ref, 0, tm)], axis=-1)
        g = jnp.dot(x, wbuf[slot, 0], preferred_element_type=F32) + bg_ref[0]
        u = jnp.dot(x, wbuf[slot, 1], preferred_element_type=F32) + bu_ref[0]
        g = jnp.minimum(g, SWIGLU_LIMIT)
        u = jnp.clip(u, -SWIGLU_LIMIT, SWIGLU_LIMIT)
        hid_ref[...] = ((u + 1.0) * (g * jax.nn.sigmoid(SWIGLU_ALPHA * g))).astype(BF16)

    _expert_weights_step(sched_ref, b, nu_ref[0], (wg_hbm, wu_hbm), wbuf, stage, sems, compute)

    @pl.when(b >= nu_ref[0])
    def _():
        hid_ref[...] = jnp.zeros(hid_ref.shape, BF16)


def _expert_scratch(n_mat, k, n):
    return [pltpu.VMEM((2, n_mat, k, n), BF16),
            pltpu.VMEM((N_STAGE, n_mat, W_CHUNK, n), F32),
            pltpu.SemaphoreType.DMA((N_STAGE, n_mat))]


def _moe_up(sched, n_used, xs, w_gate, b_gate, w_up, b_up):
    e, d, f = w_gate.shape
    n_blocks = sched.shape[0] // SCHED_W
    tm = TM_MOE
    hbm = pl.BlockSpec(memory_space=pl.ANY)
    bspec = pl.BlockSpec((1, 1, f), lambda b, sc, nu: (sc[b * SCHED_W + S_EXPERT], 0, 0))
    return pl.pallas_call(
        _moe_up_kernel,
        grid_spec=pltpu.PrefetchScalarGridSpec(
            num_scalar_prefetch=2,
            grid=(n_blocks,),
            in_specs=[pl.BlockSpec((tm * ROW_SLAB, LANES), lambda b, sc, nu: (jnp.minimum(b, nu[0] - 1), 0)),
                      hbm, bspec, hbm, bspec],
            out_specs=pl.BlockSpec((tm, f), lambda b, sc, nu: (b, 0)),
            scratch_shapes=_expert_scratch(2, d, f)),
        out_shape=jax.ShapeDtypeStruct((n_blocks * tm, f), BF16),
        compiler_params=_params(1),
        name="moe_up",
    )(sched, n_used, xs, w_gate, b_gate, w_up, b_up)


def _moe_down_kernel(sched_ref, nu_ref, hid_ref, wd_hbm, bd_ref, ys_ref, wbuf, stage, sems):
    b = pl.program_id(0)

    def compute(slot):
        out = jnp.dot(hid_ref[...], wbuf[slot, 0], preferred_element_type=F32) + bd_ref[0]
        _store_row_slabs(ys_ref, out)

    _expert_weights_step(sched_ref, b, nu_ref[0], (wd_hbm,), wbuf, stage, sems, compute)

    @pl.when(b >= nu_ref[0])
    def _():
        ys_ref[...] = jnp.zeros(ys_ref.shape, U32)


def _moe_down(sched, n_used, hid, w_down, b_down):
    e, f, d = w_down.shape
    n_blocks = sched.shape[0] // SCHED_W
    tm = TM_MOE
    return pl.pallas_call(
        _moe_down_kernel,
        grid_spec=pltpu.PrefetchScalarGridSpec(
            num_scalar_prefetch=2,
            grid=(n_blocks,),
            in_specs=[pl.BlockSpec((tm, f), lambda b, sc, nu: (b, 0)),
                      pl.BlockSpec(memory_space=pl.ANY),
                      pl.BlockSpec((1, 1, d), lambda b, sc, nu: (sc[b * SCHED_W + S_EXPERT], 0, 0))],
            out_specs=pl.BlockSpec((tm * ROW_SLAB, LANES), lambda b, sc, nu: (b, 0)),
            scratch_shapes=_expert_scratch(1, f, d)),
        out_shape=jax.ShapeDtypeStruct((n_blocks * tm * ROW_SLAB, LANES), U32),
        compiler_params=_params(1),
        name="moe_down",
    )(sched, n_used, hid, w_down, b_down)


def _combine_kernel(dest_ref, ys_hbm, x1_ref, rw_ref, g2_ref, o_ref, buf, sem):
    i = pl.program_id(0)
    n = pl.num_programs(0)
    tm = TM_COMB
    rows = TOP_K * tm

    @pl.when(i == 0)
    def _():
        _issue_row_gather(dest_ref, 0, rows, ys_hbm, buf.at[0], sem.at[0])

    for par in range(2):
        @pl.when(jnp.logical_and(i + 1 < n, (i + 1) % 2 == par))
        def _():
            _issue_row_gather(dest_ref, (i + 1) * rows, rows, ys_hbm, buf.at[par], sem.at[par])

    cur = i % 2
    _wait_row_gather(rows, ys_hbm, buf.at[cur], sem.at[cur])
    rw = rw_ref[...]
    y = [jnp.zeros((tm, LANES), F32)] * (2 * ROW_SLAB)
    for k in range(TOP_K):
        w_k = rw[:, k:k + 1]
        y = [yc + w_k * c for yc, c in zip(y, _load_row_slabs(buf.at[cur], k * tm, tm))]
    o_ref[...] = x1_ref[...] + g2_ref[0] * jnp.concatenate(y, axis=-1)


def _combine(dest_km, ys, x1, rw, g2, seq):
    t, d = x1.shape
    tm = TM_COMB
    per_seq = seq // tm
    return pl.pallas_call(
        _combine_kernel,
        grid_spec=pltpu.PrefetchScalarGridSpec(
            num_scalar_prefetch=1,
            grid=(t // tm,),
            in_specs=[pl.BlockSpec(memory_space=pl.ANY),
                      pl.BlockSpec((tm, d), lambda i, dr: (i, 0)),
                      pl.BlockSpec((tm, LANES), lambda i, dr: (i, 0)),
                      pl.BlockSpec((1, 1, d), lambda i, dr: (i // per_seq, 0, 0))],
            out_specs=pl.BlockSpec((tm, d), lambda i, dr: (i, 0)),
            scratch_shapes=[pltpu.VMEM((2, TOP_K * tm * ROW_SLAB, LANES), U32),
                            pltpu.SemaphoreType.DMA((2,))]),
        out_shape=jax.ShapeDtypeStruct((t, d), F32),
        compiler_params=_params(1),
        name="combine",
    )(dest_km, ys, x1, rw, g2)


def _pad_cols(w, n):
    return jnp.pad(w, ((0, 0), (0, n - w.shape[1])))


def _layer(x, mod, norm_mix_g, norm_ffn_g, w_in, b_f, q_norm_g, k_norm_g, conv_w, conv_b,
           conv_ln_g, conv_ln_b, w_out, w_router, b_router, w_gate, b_gate, w_up, b_up,
           w_down, b_down):
    b, s, d = x.shape
    t = b * s
    shift1, scale1, gate1, shift2, scale2, gate2 = [m[:, None, :] for m in jnp.split(mod, 6, axis=-1)]
    a1 = norm_mix_g[None, None, :] * (1.0 + scale1)
    a2 = norm_ffn_g[None, None, :] * (1.0 + scale2)

    aw = ATTN_WIDTH
    w_qkv = w_in[:, :3 * aw].astype(BF16)
    w_conv = w_in[:, 3 * aw + ATTN_HEADS:].astype(BF16)
    w_f = _pad_cols(w_in[:, 3 * aw:3 * aw + ATTN_HEADS], LANES).astype(BF16)
    b_f_pad = _pad_cols(b_f[None, :], LANES)

    q, k, v, u, logf = _in_proj(x, a1, shift1, w_qkv, w_conv, w_f, b_f_pad,
                                q_norm_g[None, :], k_norm_g[None, :])
    attn = _fox_attention(q, k, v, _forget_cumsum(logf))
    conv = _conv_module(u, conv_w, conv_b[None, :], conv_ln_g[None, :], conv_ln_b[None, :])

    w_r = _pad_cols(w_router, LANES).astype(BF16)
    b_r = jnp.concatenate([b_router, jnp.full((LANES - N_EXPERTS,), NEG_BIG, F32)])[None, :]
    x1, z, logits = _out_proj(attn, conv, x, w_out.astype(BF16), gate1, a2, shift2, w_r, b_r)

    ri, rr, rw, cnt = _route(logits.reshape(t, LANES))

    tm = TM_MOE
    n_blocks = t * TOP_K // tm + N_EXPERTS
    counts = cnt[0, :N_EXPERTS].astype(I32)
    padded = (counts + tm - 1) // tm * tm
    padded_end = jnp.cumsum(padded)
    padded_start = padded_end - padded
    idx_t = ri[:, :TOP_K].T
    start_of = jnp.zeros_like(idx_t)
    for e in range(N_EXPERTS):
        start_of = jnp.where(idx_t == e, padded_start[e], start_of)
    dest_t = start_of + rr[:, :TOP_K].T
    block_start = jnp.arange(n_blocks, dtype=I32)[:, None] * tm
    block_e = jnp.minimum(jnp.sum((padded_end[None, :] <= block_start).astype(I32), axis=1), N_EXPERTS - 1)
    n_used = (padded_end[-1:] // tm).astype(I32)
    dest_km = dest_t.reshape(TOP_K, t // TM_COMB, TM_COMB).transpose(1, 0, 2).reshape(-1)

    pad_table = jnp.concatenate([padded_start + counts, padded - counts, n_used])
    xs = _dispatch(dest_t.T.reshape(-1), pad_table, z, n_blocks * tm)

    experts = jnp.arange(N_EXPERTS, dtype=I32)
    has = counts > 0
    later = (experts[None, :] > experts[:, None]) & has[None, :]
    next_e = jnp.min(jnp.where(later, experts[None, :], N_EXPERTS), axis=1)
    run_slot = (jnp.cumsum(has.astype(I32)) - 1) % 2
    of_block = block_e[:, None] == experts[None, :]

    def per_block(table):
        return jnp.sum(jnp.where(of_block, table[None, :], 0), axis=1)

    run_len = per_block(jnp.maximum(padded // tm, 1))
    pos = jnp.arange(n_blocks, dtype=I32) - per_block(padded_start // tm)
    nxt = per_block(next_e)
    live = nxt < N_EXPERTS
    c0 = jnp.where(live, N_WCHUNK * pos // run_len, 0)
    c1 = jnp.where(live, N_WCHUNK * (pos + 1) // run_len, 0)
    sched = jnp.stack([block_e, per_block(run_slot), jnp.where(live, nxt, 0), c0, c1], axis=1).reshape(-1)

    hid = _moe_up(sched, n_used, xs, w_gate, b_gate[:, None, :], w_up, b_up[:, None, :])
    ys = _moe_down(sched, n_used, hid, w_down, b_down[:, None, :])
    out = _combine(dest_km, ys, x1.reshape(t, d), rw, gate2, s)
    return out.reshape(b, s, d)


def kernel(x, c, ada_w, ada_b, norm_mix_g, norm_ffn_g, w_in, b_f, q_norm_g, k_norm_g, conv_w, conv_b,
           conv_ln_g, conv_ln_b, w_out, w_router, b_router, w_gate, b_gate, w_up, b_up, w_down, b_down):
    b = x.shape[0]
    c_pad = jnp.pad(c, ((0, SUBLANES - b), (0, 0)))
    for l in range(ada_w.shape[0]):
        mod = _ada_mod(c_pad, ada_w[l], ada_b[l])[:b]
        x = _layer(x, mod, norm_mix_g[l], norm_ffn_g[l], w_in[l], b_f[l], q_norm_g[l], k_norm_g[l],
                   conv_w[l], conv_b[l], conv_ln_g[l], conv_ln_b[l], w_out[l], w_router[l],
                   b_router[l], w_gate[l], b_gate[l], w_up[l], b_up[l], w_down[l], b_down[l])
    return x
```
